```python
import jax, jax.numpy as jnp
from jax import lax
import numpy as np

D_MODEL = 1024
BATCH = 4
SEQ = 8192
DEPTH = 1
DEC_BATCH = 16
DEC_SEQ = 16
PAST_LEN = 1024

CHUNK = 64
WINDOW = 128
W_CHUNKS = WINDOW // CHUNK
HEAD_DIM = 64
N_HEADS = D_MODEL // HEAD_DIM
N_KV_HEADS = 4
GROUP = N_HEADS // N_KV_HEADS
Q_DIM = N_HEADS * HEAD_DIM
KV_DIM = N_KV_HEADS * HEAD_DIM
D_CONV = D_MODEL
CONV_W = 3
D_FF = ((8 * D_MODEL // 3 + 255) // 256) * 256
EPS = 1e-6
NEG_INF = -1e30
SCALE = HEAD_DIM ** -0.5
SPLITS = list(np.cumsum([D_CONV, D_CONV, D_CONV, Q_DIM, KV_DIM, KV_DIM, D_MODEL]))
IN_COLS = 3 * D_CONV + Q_DIM + 2 * KV_DIM + 2 * D_MODEL

kernel_name = "hybrid_shortconv_swa_sink_stream_step"


def rms_norm(x, w):
    xf = x.astype(jnp.float32)
    y = xf * lax.rsqrt(jnp.mean(xf * xf, axis=-1, keepdims=True) + EPS)
    return (y * w.astype(jnp.float32)).astype(x.dtype)


def mixer_inputs(x, norm_w, w_in, q_norm_w, k_norm_w):
    b, t = x.shape[:2]
    xn = rms_norm(x, norm_w)
    h, gb, gc, q, k, v, g_conv, g_attn = jnp.split(xn @ w_in, SPLITS, axis=-1)
    q = rms_norm(q.reshape(b, t, N_KV_HEADS, GROUP, HEAD_DIM), q_norm_w)
    k = rms_norm(k.reshape(b, t, N_KV_HEADS, HEAD_DIM), k_norm_w)
    v = v.reshape(b, t, N_KV_HEADS, HEAD_DIM)
    u = gc * h
    return u, gb, q, k, v, g_conv, g_attn


def causal_conv(u, prefix, w):
    t = u.shape[1]
    up = jnp.concatenate([prefix, u], axis=1)
    y = w[0] * up[:, 0:t] + w[1] * up[:, 1:t + 1] + w[2] * up[:, 2:t + 2]
    return y, up[:, -(CONV_W - 1):]


def sink_attention(q, k, v, valid, sinks):
    s = jnp.einsum('bcqhgd,bckhd->bchgqk', q, k).astype(jnp.float32) * SCALE
    s = jnp.where(valid, s, NEG_INF)
    sink = jnp.broadcast_to(sinks.astype(jnp.float32).reshape(N_KV_HEADS, GROUP)[None, None, :, :, None, None],
                            s.shape[:-1] + (1,))
    p = jax.nn.softmax(jnp.concatenate([s, sink], axis=-1), axis=-1)[..., :-1]
    return jnp.einsum('bchgqk,bckhd->bcqhgd', p.astype(v.dtype), v)


def window_attention_prompt(q, k, v, sinks):
    b, t = q.shape[:2]
    nc = t // CHUNK
    qc = q.reshape(b, nc, CHUNK, N_KV_HEADS, GROUP, HEAD_DIM)

    def band(a):
        ac = a.reshape(b, nc, CHUNK, N_KV_HEADS, HEAD_DIM)
        ap = jnp.pad(ac, ((0, 0), (W_CHUNKS, 0), (0, 0), (0, 0), (0, 0)))
        return jnp.concatenate([ap[:, j:j + nc] for j in range(W_CHUNKS + 1)], axis=2)

    kb, vb = band(k), band(v)
    chunk_idx = jnp.arange(nc)[:, None]
    src_off = jnp.repeat(jnp.arange(W_CHUNKS + 1), CHUNK)[None, :]
    valid = (chunk_idx + src_off - W_CHUNKS >= 0)[None, :, None, None, None, :]
    o = sink_attention(qc, kb, vb, valid, sinks)
    return o.reshape(b, t, Q_DIM)


def window_attention_sample(q, k, v, cache_k, cache_v, sinks):
    b, t = q.shape[:2]
    kk = jnp.concatenate([cache_k, k], axis=1)[:, None]
    vv = jnp.concatenate([cache_v, v], axis=1)[:, None]
    o = sink_attention(q[:, None], kk, vv, jnp.ones((), dtype=bool), sinks)
    return o[:, 0].reshape(b, t, Q_DIM)


def block_output(x, y_conv, y_attn, g_conv, g_attn, w_out, norm_ffn_w, w_gate, w_up, w_down):
    merged = jax.nn.sigmoid(g_conv) * y_conv + jax.nn.sigmoid(g_attn) * y_attn
    x = x + merged @ w_out
    xn = rms_norm(x, norm_ffn_w)
    return x + (jax.nn.silu(xn @ w_gate) * (xn @ w_up)) @ w_down


def setup_inputs(seed: int = 0) -> dict:
    key = jax.random.key(seed)
    ks = jax.random.split(key, 18)
    f32 = jnp.float32
    cache_len = min(WINDOW, PAST_LEN)
    nrm = lambda k, shape, s: jax.random.normal(k, shape, f32) * s
    return {
        "x_prompt": nrm(ks[0], (BATCH, SEQ, D_MODEL), 1.0),
        "x_sample": nrm(ks[1], (DEC_BATCH, DEC_SEQ, D_MODEL), 1.0),
        "state_conv": nrm(ks[2], (DEPTH, DEC_BATCH, CONV_W - 1, D_CONV), 1.0),
        "cache_k": nrm(ks[3], (DEPTH, DEC_BATCH, cache_len, N_KV_HEADS, HEAD_DIM), 1.0),
        "cache_v": nrm(ks[4], (DEPTH, DEC_BATCH, cache_len, N_KV_HEADS, HEAD_DIM), 1.0),
        "norm_mix_w": 1.0 + nrm(ks[5], (DEPTH, D_MODEL), 0.02),
        "w_in": nrm(ks[6], (DEPTH, D_MODEL, IN_COLS), D_MODEL ** -0.5),
        "conv_w": nrm(ks[7], (DEPTH, CONV_W, D_CONV), CONV_W ** -0.5),
        "q_norm_w": 1.0 + nrm(ks[8], (DEPTH, HEAD_DIM), 0.02),
        "k_norm_w": 1.0 + nrm(ks[9], (DEPTH, HEAD_DIM), 0.02),
        "sinks": nrm(ks[10], (DEPTH, N_HEADS), 0.5),
        "w_out": nrm(ks[11], (DEPTH, D_MODEL, D_MODEL), D_MODEL ** -0.5),
        "norm_ffn_w": 1.0 + nrm(ks[12], (DEPTH, D_MODEL), 0.02),
        "w_gate": nrm(ks[13], (DEPTH, D_MODEL, D_FF), D_MODEL ** -0.5),
        "w_up": nrm(ks[14], (DEPTH, D_MODEL, D_FF), D_MODEL ** -0.5),
        "w_down": nrm(ks[15], (DEPTH, D_FF, D_MODEL), D_FF ** -0.5),
    }


def reference(x_prompt, x_sample, state_conv, cache_k, cache_v, norm_mix_w, w_in, conv_w,
              q_norm_w, k_norm_w, sinks, w_out, norm_ffn_w, w_gate, w_up, w_down):
    yp, ys = x_prompt, x_sample
    conv_p, k_p, v_p, conv_s, k_s, v_s = [], [], [], [], [], []
    for l in range(DEPTH):
        u, gb, q, k, v, g_c, g_a = mixer_inputs(yp, norm_mix_w[l], w_in[l], q_norm_w[l], k_norm_w[l])
        zero_prefix = jnp.zeros((u.shape[0], CONV_W - 1, D_CONV), u.dtype)
        cy, ctail = causal_conv(u, zero_prefix, conv_w[l])
        ay = window_attention_prompt(q, k, v, sinks[l])
        yp = block_output(yp, gb * cy, ay, g_c, g_a, w_out[l], norm_ffn_w[l], w_gate[l], w_up[l], w_down[l])
        conv_p.append(ctail)
        k_p.append(k[:, -WINDOW:])
        v_p.append(v[:, -WINDOW:])
        u, gb, q, k, v, g_c, g_a = mixer_inputs(ys, norm_mix_w[l], w_in[l], q_norm_w[l], k_norm_w[l])
        cy, ctail = causal_conv(u, state_conv[l].astype(u.dtype), conv_w[l])
        ay = window_attention_sample(q, k, v, cache_k[l].astype(k.dtype), cache_v[l].astype(v.dtype), sinks[l])
        ys = block_output(ys, gb * cy, ay, g_c, g_a, w_out[l], norm_ffn_w[l], w_gate[l], w_up[l], w_down[l])
        conv_s.append(ctail)
        k_s.append(k)
        v_s.append(v)
    return (yp, ys, jnp.stack(conv_p), jnp.stack(k_p), jnp.stack(v_p),
            jnp.stack(conv_s), jnp.stack(k_s), jnp.stack(v_s))
```

```python
import functools

import jax
import jax.numpy as jnp
from jax import lax
from jax.experimental import pallas as pl
from jax.experimental.pallas import tpu as pltpu

D_MODEL = 1024
HEAD_DIM = 64
N_HEADS = 16
N_KV_HEADS = 4
GROUP = N_HEADS // N_KV_HEADS
KV_DIM = N_KV_HEADS * HEAD_DIM
D_CONV = D_MODEL
CONV_W = 3
D_FF = 2816
CHUNK = 64
WINDOW = 128
EPS = 1e-6
NEG_INF = -1e30
SCALE = HEAD_DIM ** -0.5

LANES = 128
SUBLANES = 8
MXU_DIM = 256
VMEM_LIMIT_BYTES = 56 * 1024 * 1024

COL_BLOCK = MXU_DIM
N_COL_BLOCKS = D_CONV // COL_BLOCK
ATT_TOK = 2 * CHUNK
ATT_KEYS = ATT_TOK + WINDOW
FF_BLOCKS = ((0, 1024), (1024, 2048), (2048, D_FF))

BF16 = jnp.bfloat16
F32 = jnp.float32


def _rms_scale(x):
    return lax.rsqrt(jnp.mean(x * x, axis=-1, keepdims=True) + EPS)


def _mixer_in_kernel(x_ref, pre_ref, nw_ref, w1_ref, w2_ref, cw_ref, bd_ref, qw_ref, kw_ref,
                     ycg_ref, qs_ref, sga_ref, kh_ref, v_ref, cnew_ref, s_ref,
                     *, seqs_per_tile, rows_per_seq, tiles_per_seq):
    j = pl.program_id(1)
    slot = rows_per_seq + SUBLANES
    x = x_ref[...]
    xn = (x * _rms_scale(x) * nw_ref[...]).astype(BF16)

    def load_prefix():
        for b in range(seqs_per_tile):
            s_ref[b * slot + 6:b * slot + 8, :] = pre_ref[b]

    if tiles_per_seq == 1:
        load_prefix()
    else:
        pl.when(j == 0)(load_prefix)

    for cb in range(N_COL_BLOCKS):
        cs = slice(cb * COL_BLOCK, (cb + 1) * COL_BLOCK)
        r = jnp.dot(xn, w1_ref[cb], preferred_element_type=F32)
        h = r[:, 0 * COL_BLOCK:1 * COL_BLOCK]
        gb = r[:, 1 * COL_BLOCK:2 * COL_BLOCK]
        gc = r[:, 2 * COL_BLOCK:3 * COL_BLOCK]
        gcv = r[:, 3 * COL_BLOCK:4 * COL_BLOCK]
        u = gc * h
        for b in range(seqs_per_tile):
            rs = slice(b * rows_per_seq, (b + 1) * rows_per_seq)
            base = b * slot + SUBLANES
            ub = u[rs]
            s_ref[base:base + rows_per_seq, cs] = ub
            um1 = s_ref[base - 1:base - 1 + rows_per_seq, cs]
            um2 = s_ref[base - 2:base - 2 + rows_per_seq, cs]
            y = cw_ref[0:1, cs] * um2 + cw_ref[1:2, cs] * um1 + cw_ref[2:3, cs] * ub
            ycg_ref[rs, cs] = (jax.nn.sigmoid(gcv[rs]) * (gb[rs] * y)).astype(BF16)

    def store_tail():
        for b in range(seqs_per_tile):
            end = b * slot + SUBLANES + rows_per_seq
            cnew_ref[b] = s_ref[end - 2:end, :]

    if tiles_per_seq == 1:
        store_tail()
    else:
        pl.when(j == tiles_per_seq - 1)(store_tail)
        end = SUBLANES + rows_per_seq
        s_ref[6:8, :] = s_ref[end - 2:end, :]

    r2 = jnp.dot(xn, w2_ref[...], preferred_element_type=F32)
    bd = bd_ref[...]
    inv_hd = 1.0 / HEAD_DIM

    def head_scale(t):
        ss = jnp.dot((t * t).astype(BF16), bd, preferred_element_type=F32)
        return lax.rsqrt(ss * inv_hd + EPS)

    for c in range(D_MODEL // MXU_DIM):
        cs = slice(c * MXU_DIM, (c + 1) * MXU_DIM)
        qb = r2[:, cs]
        qs_ref[:, cs] = ((qb * head_scale(qb) * qw_ref[:, cs]) * SCALE).astype(BF16)
    sga_ref[...] = jax.nn.sigmoid(r2[:, D_MODEL:2 * D_MODEL]).astype(BF16)
    k = r2[:, 2 * D_MODEL:2 * D_MODEL + KV_DIM]
    kh_ref[...] = k * head_scale(k) * kw_ref[...]
    v_ref[...] = r2[:, 2 * D_MODEL + KV_DIM:2 * D_MODEL + 2 * KV_DIM]


def _mixer_in(x2d, prefix, nw, w1, w2, cw, bd, qw, kw, *, tile, seq_len):
    n = x2d.shape[0]
    n_seq = n // seq_len
    if tile >= seq_len:
        seqs_per_tile, rows_per_seq, tiles_per_seq = tile // seq_len, seq_len, 1
    else:
        seqs_per_tile, rows_per_seq, tiles_per_seq = 1, tile, seq_len // tile
    assert seqs_per_tile * rows_per_seq == tile and n % tile == 0
    grid = (n_seq // seqs_per_tile, tiles_per_seq)
    row = lambda i, j: (i * tiles_per_seq + j, 0)
    const2 = lambda i, j: (0, 0)
    const3 = lambda i, j: (0, 0, 0)
    seq3 = lambda i, j: (i, 0, 0)
    resident = dict(pipeline_mode=pl.Buffered(1))
    kern = functools.partial(_mixer_in_kernel, seqs_per_tile=seqs_per_tile,
                             rows_per_seq=rows_per_seq, tiles_per_seq=tiles_per_seq)
    return pl.pallas_call(
        kern,
        grid=grid,
        in_specs=[
            pl.BlockSpec((tile, D_MODEL), row),
            pl.BlockSpec((seqs_per_tile, CONV_W - 1, D_CONV), seq3),
            pl.BlockSpec((1, D_MODEL), const2),
            pl.BlockSpec(w1.shape, const3, **resident),
            pl.BlockSpec(w2.shape, const2, **resident),
            pl.BlockSpec(cw.shape, const2),
            pl.BlockSpec(bd.shape, const2),
            pl.BlockSpec(qw.shape, const2),
            pl.BlockSpec(kw.shape, const2),
        ],
        out_specs=[
            pl.BlockSpec((tile, D_CONV), row),
            pl.BlockSpec((tile, D_MODEL), row),
            pl.BlockSpec((tile, D_MODEL), row),
            pl.BlockSpec((tile, KV_DIM), row),
            pl.BlockSpec((tile, KV_DIM), row),
            pl.BlockSpec((seqs_per_tile, CONV_W - 1, D_CONV), seq3),
        ],
        out_shape=[
            jax.ShapeDtypeStruct((n, D_CONV), BF16),
            jax.ShapeDtypeStruct((n, D_MODEL), BF16),
            jax.ShapeDtypeStruct((n, D_MODEL), BF16),
            jax.ShapeDtypeStruct((n, KV_DIM), F32),
            jax.ShapeDtypeStruct((n, KV_DIM), F32),
            jax.ShapeDtypeStruct((n_seq, CONV_W - 1, D_CONV), F32),
        ],
        scratch_shapes=[pltpu.VMEM((seqs_per_tile * (rows_per_seq + SUBLANES), D_CONV), F32)],
        compiler_params=pltpu.CompilerParams(
            dimension_semantics=("arbitrary", "arbitrary"),
            vmem_limit_bytes=VMEM_LIMIT_BYTES),
        name="mixer_in",
    )(x2d, prefix, nw, w1, w2, cw, bd, qw, kw)


def _attention_kernel(q_ref, kp_ref, kc_ref, vp_ref, vc_ref, mask_ref, sink_ref, ycg_ref, sga_ref,
                      o_ref):
    kb = jnp.concatenate([kp_ref[...], kc_ref[...]], axis=0).astype(BF16)
    vcat = jnp.concatenate([vp_ref[...], vc_ref[...]], axis=0)
    vt = vcat.T.astype(BF16)
    valid = mask_ref[0] != 0.0
    zeros = jnp.zeros((HEAD_DIM, ATT_TOK), BF16)
    heads_per_block = LANES // HEAD_DIM
    for p in range(D_MODEL // LANES):
        cs = slice(p * LANES, (p + 1) * LANES)
        qt = q_ref[:, cs].astype(F32).T
        outs = []
        for half in range(heads_per_block):
            head = p * heads_per_block + half
            kv = head // GROUP
            qh = qt[half * HEAD_DIM:(half + 1) * HEAD_DIM].astype(BF16)
            rhs = jnp.concatenate([qh, zeros] if kv % 2 == 0 else [zeros, qh], axis=0)
            slab = (kv // 2) * LANES
            s = jnp.dot(kb[:, slab:slab + LANES], rhs, preferred_element_type=F32)
            s = jnp.where(valid, s, NEG_INF)
            sink = sink_ref[head:head + 1, :]
            m = jnp.maximum(jnp.max(s, axis=0, keepdims=True), sink)
            e = jnp.exp(s - m)
            den = jnp.sum(e, axis=0, keepdims=True) + jnp.exp(sink - m)
            o = jnp.dot(vt[kv * HEAD_DIM:(kv + 1) * HEAD_DIM, :], e.astype(BF16),
                        preferred_element_type=F32)
            outs.append(o / den)
        a = jnp.concatenate(outs, axis=0).T
        o_ref[:, cs] = (ycg_ref[:, cs].astype(F32) + sga_ref[:, cs].astype(F32) * a).astype(BF16)


def _attention(qs, k_prev_arr, k_cur_arr, v_prev_arr, v_cur_arr, mask, sink_b, ycg, sga,
               *, n_seq, groups_per_seq, prev_map, mask_map):
    n = qs.shape[0]
    cur = lambda i, j: (i * groups_per_seq + j, 0)
    const2 = lambda i, j: (0, 0)
    return pl.pallas_call(
        _attention_kernel,
        grid=(n_seq, groups_per_seq),
        in_specs=[
            pl.BlockSpec((ATT_TOK, D_MODEL), cur),
            pl.BlockSpec((WINDOW, KV_DIM), prev_map),
            pl.BlockSpec((ATT_TOK, KV_DIM), cur),
            pl.BlockSpec((WINDOW, KV_DIM), prev_map),
            pl.BlockSpec((ATT_TOK, KV_DIM), cur),
            pl.BlockSpec((1, ATT_KEYS, ATT_TOK), mask_map),
            pl.BlockSpec(sink_b.shape, const2),
            pl.BlockSpec((ATT_TOK, D_MODEL), cur),
            pl.BlockSpec((ATT_TOK, D_MODEL), cur),
        ],
        out_specs=pl.BlockSpec((ATT_TOK, D_MODEL), cur),
        out_shape=jax.ShapeDtypeStruct((n, D_MODEL), BF16),
        compiler_params=pltpu.CompilerParams(
            dimension_semantics=("arbitrary", "arbitrary"),
            vmem_limit_bytes=VMEM_LIMIT_BYTES),
        name="attention",
    )(qs, k_prev_arr, k_cur_arr, v_prev_arr, v_cur_arr, mask, sink_b, ycg, sga)


def _out_ffn_kernel(m_ref, x_ref, wo_ref, nw_ref, wg_ref, wu_ref, wd_ref, y_ref):
    x1 = x_ref[...] + jnp.dot(m_ref[...], wo_ref[...], preferred_element_type=F32)
    xn = (x1 * _rms_scale(x1) * nw_ref[...]).astype(BF16)
    acc = x1
    for lo, hi in FF_BLOCKS:
        g = jnp.dot(xn, wg_ref[:, lo:hi], preferred_element_type=F32)
        u = jnp.dot(xn, wu_ref[:, lo:hi], preferred_element_type=F32)
        a = (jax.nn.silu(g) * u).astype(BF16)
        acc = acc + jnp.dot(a, wd_ref[lo:hi, :], preferred_element_type=F32)
    y_ref[...] = acc


def _out_ffn(merged, x2d, wo, nw, wg, wu, wd, *, tile):
    n = x2d.shape[0]
    row = lambda i: (i, 0)
    const2 = lambda i: (0, 0)
    resident = dict(pipeline_mode=pl.Buffered(1))
    return pl.pallas_call(
        _out_ffn_kernel,
        grid=(n // tile,),
        in_specs=[
            pl.BlockSpec((tile, D_MODEL), row),
            pl.BlockSpec((tile, D_MODEL), row),
            pl.BlockSpec(wo.shape, const2, **resident),
            pl.BlockSpec((1, D_MODEL), const2),
            pl.BlockSpec(wg.shape, const2, **resident),
            pl.BlockSpec(wu.shape, const2, **resident),
            pl.BlockSpec(wd.shape, const2, **resident),
        ],
        out_specs=pl.BlockSpec((tile, D_MODEL), row),
        out_shape=jax.ShapeDtypeStruct((n, D_MODEL), F32),
        compiler_params=pltpu.CompilerParams(
            dimension_semantics=("arbitrary",),
            vmem_limit_bytes=VMEM_LIMIT_BYTES),
        name="out_ffn",
    )(merged, x2d, wo, nw, wg, wu, wd)


def _prompt_mask():
    key_blk = lax.broadcasted_iota(jnp.int32, (ATT_KEYS, ATT_TOK), 0) // CHUNK
    tok_blk = lax.broadcasted_iota(jnp.int32, (ATT_KEYS, ATT_TOK), 1) // CHUNK
    band = (key_blk >= tok_blk) & (key_blk <= tok_blk + WINDOW // CHUNK)
    first = band & (key_blk >= WINDOW // CHUNK)
    return jnp.stack([first, band]).astype(F32)


def _pad_rows(a, n_seq, rows, to_rows):
    a = a.reshape(n_seq, rows, a.shape[-1])
    a = jnp.pad(a, ((0, 0), (0, to_rows - rows), (0, 0)))
    return a.reshape(n_seq * to_rows, a.shape[-1])


def _layer(xp, xs, state_conv, cache_k, cache_v, norm_mix_w, w_in, conv_w, q_norm_w, k_norm_w,
           sinks, w_out, norm_ffn_w, w_gate, w_up, w_down):
    b_p, t_p, _ = xp.shape
    b_s, t_s, _ = xs.shape

    hs, bs, cs_, qs_, ks_, vs_, gcs, gas = (0, D_CONV, 2 * D_CONV, 3 * D_CONV, 3 * D_CONV + D_MODEL,
                                            3 * D_CONV + D_MODEL + KV_DIM,
                                            3 * D_CONV + D_MODEL + 2 * KV_DIM,
                                            4 * D_CONV + D_MODEL + 2 * KV_DIM)
    w_in_b = w_in.astype(BF16)
    w1 = jnp.stack([
        jnp.concatenate([w_in_b[:, s + cb * COL_BLOCK:s + (cb + 1) * COL_BLOCK] for s in (hs, bs, cs_, gcs)],
                        axis=1)
        for cb in range(N_COL_BLOCKS)])
    w2 = jnp.concatenate([w_in_b[:, qs_:qs_ + D_MODEL], w_in_b[:, gas:gas + D_MODEL],
                          w_in_b[:, ks_:ks_ + KV_DIM], w_in_b[:, vs_:vs_ + KV_DIM]], axis=1)
    head_id = jnp.arange(MXU_DIM) // HEAD_DIM
    bd = (head_id[:, None] == head_id[None, :]).astype(BF16)
    nw = norm_mix_w.reshape(1, D_MODEL)
    qw = jnp.tile(q_norm_w, N_HEADS).reshape(1, D_MODEL)
    kw = jnp.tile(k_norm_w, N_KV_HEADS).reshape(1, KV_DIM)
    sink_b = jnp.broadcast_to(sinks.astype(F32)[:, None], (N_HEADS, ATT_TOK))
    wo = w_out.astype(BF16)
    nfw = norm_ffn_w.reshape(1, D_MODEL)
    wg, wu, wd = w_gate.astype(BF16), w_up.astype(BF16), w_down.astype(BF16)

    xp2 = xp.reshape(b_p * t_p, D_MODEL)
    zero_prefix = jnp.zeros((b_p, CONV_W - 1, D_CONV), F32)
    ycg, q, sga, kh, v, cnew_p = _mixer_in(xp2, zero_prefix, nw, w1, w2, conv_w, bd, qw, kw,
                                           tile=512, seq_len=t_p)
    gps = t_p // ATT_TOK
    merged = _attention(
        q, kh, kh, v, v, _prompt_mask(), sink_b, ycg, sga, n_seq=b_p, groups_per_seq=gps,
        prev_map=lambda i, j: (i * gps + jnp.maximum(j - 1, 0), 0),
        mask_map=lambda i, j: (jnp.minimum(j, 1), 0, 0))
    yp = _out_ffn(merged, xp2, wo, nfw, wg, wu, wd, tile=512).reshape(b_p, t_p, D_MODEL)
    k_p = kh.reshape(b_p, t_p, N_KV_HEADS, HEAD_DIM)[:, -WINDOW:]
    v_p = v.reshape(b_p, t_p, N_KV_HEADS, HEAD_DIM)[:, -WINDOW:]

    xs2 = xs.reshape(b_s * t_s, D_MODEL)
    ycg_s, q_s, sga_s, kh_s, v_s, cnew_s = _mixer_in(xs2, state_conv, nw, w1, w2, conv_w, bd, qw, kw,
                                                     tile=b_s * t_s, seq_len=t_s)
    pad = functools.partial(_pad_rows, n_seq=b_s, rows=t_s, to_rows=ATT_TOK)
    cache_len = cache_k.shape[1]
    sample_mask = (lax.broadcasted_iota(jnp.int32, (1, ATT_KEYS, ATT_TOK), 1)
                   < cache_len + t_s).astype(F32)
    merged_s = _attention(
        pad(q_s), cache_k.reshape(b_s * cache_len, KV_DIM), pad(kh_s),
        cache_v.reshape(b_s * cache_len, KV_DIM), pad(v_s), sample_mask, sink_b, pad(ycg_s), pad(sga_s),
        n_seq=b_s, groups_per_seq=1,
        prev_map=lambda i, j: (i, 0), mask_map=lambda i, j: (0, 0, 0))
    merged_s = merged_s.reshape(b_s, ATT_TOK, D_MODEL)[:, :t_s].reshape(b_s * t_s, D_MODEL)
    ys = _out_ffn(merged_s, xs2, wo, nfw, wg, wu, wd, tile=b_s * t_s).reshape(b_s, t_s, D_MODEL)
    k_s = kh_s.reshape(b_s, t_s, N_KV_HEADS, HEAD_DIM)
    v_s = v_s.reshape(b_s, t_s, N_KV_HEADS, HEAD_DIM)
    return yp, ys, cnew_p, k_p, v_p, cnew_s, k_s, v_s


def kernel(x_prompt, x_sample, state_conv, cache_k, cache_v, norm_mix_w, w_in, conv_w, q_norm_w, k_norm_w, sinks, w_out, norm_ffn_w, w_gate, w_up, w_down):
    depth = w_in.shape[0]
    assert cache_k.shape[2] == WINDOW, "sample attention assumes a full window of cached rows"
    yp, ys = x_prompt, x_sample
    outs = [[] for _ in range(6)]
    for l in range(depth):
        yp, ys, *rest = _layer(
            yp, ys, state_conv[l], cache_k[l], cache_v[l], norm_mix_w[l], w_in[l], conv_w[l],
            q_norm_w[l], k_norm_w[l], sinks[l], w_out[l], norm_ffn_w[l], w_gate[l], w_up[l], w_down[l])
        for acc, val in zip(outs, rest):
            acc.append(val)
    return (yp, ys) + tuple(jnp.stack(o) for o in outs)
```

```python
import functools

import jax
import jax.numpy as jnp
from jax import lax
from jax.experimental import pallas as pl
from jax.experimental.pallas import tpu as pltpu

D_MODEL = 1024
HEAD_DIM = 64
N_HEADS = 16
N_KV_HEADS = 4
GROUP = N_HEADS // N_KV_HEADS
KV_DIM = N_KV_HEADS * HEAD_DIM
D_CONV = D_MODEL
CONV_W = 3
D_FF = 2816
CHUNK = 64
WINDOW = 128
EPS = 1e-6
NEG_INF = -1e30
SCALE = HEAD_DIM ** -0.5

LANES = 128
SUBLANES = 8
MXU_DIM = 256
VMEM_LIMIT_BYTES = 56 * 1024 * 1024

COL_BLOCK = MXU_DIM
N_COL_BLOCKS = D_CONV // COL_BLOCK
ATT_TOK = 2 * CHUNK
ATT_KEYS = ATT_TOK + WINDOW
FF_BLOCKS = ((0, 1024), (1024, 2048), (2048, D_FF))

BF16 = jnp.bfloat16
F32 = jnp.float32


def _rms_scale(x):
    return lax.rsqrt(jnp.mean(x * x, axis=-1, keepdims=True) + EPS)


def _mixer_in_kernel(x_ref, pre_ref, nw_ref, w1_ref, w2_ref, cw_ref, bd_ref, qw_ref, kw_ref,
                     ycg_ref, qs_ref, sga_ref, kh_ref, v_ref, cnew_ref, s_ref,
                     *, seqs_per_tile, rows_per_seq, tiles_per_seq):
    j = pl.program_id(1)
    slot = rows_per_seq + SUBLANES
    x = x_ref[...]
    xn = (x * _rms_scale(x) * nw_ref[...]).astype(BF16)

    def load_prefix():
        for b in range(seqs_per_tile):
            s_ref[b * slot + 6:b * slot + 8, :] = pre_ref[b]

    if tiles_per_seq == 1:
        load_prefix()
    else:
        pl.when(j == 0)(load_prefix)

    for cb in range(N_COL_BLOCKS):
        cs = slice(cb * COL_BLOCK, (cb + 1) * COL_BLOCK)
        r = jnp.dot(xn, w1_ref[cb], preferred_element_type=F32)
        h = r[:, 0 * COL_BLOCK:1 * COL_BLOCK]
        gb = r[:, 1 * COL_BLOCK:2 * COL_BLOCK]
        gc = r[:, 2 * COL_BLOCK:3 * COL_BLOCK]
        gcv = r[:, 3 * COL_BLOCK:4 * COL_BLOCK]
        u = gc * h
        for b in range(seqs_per_tile):
            rs = slice(b * rows_per_seq, (b + 1) * rows_per_seq)
            base = b * slot + SUBLANES
            ub = u[rs]
            s_ref[base:base + rows_per_seq, cs] = ub
            um1 = s_ref[base - 1:base - 1 + rows_per_seq, cs]
            um2 = s_ref[base - 2:base - 2 + rows_per_seq, cs]
            y = cw_ref[0:1, cs] * um2 + cw_ref[1:2, cs] * um1 + cw_ref[2:3, cs] * ub
            ycg_ref[rs, cs] = (jax.nn.sigmoid(gcv[rs]) * (gb[rs] * y)).astype(BF16)

    def store_tail():
        for b in range(seqs_per_tile):
            end = b * slot + SUBLANES + rows_per_seq
            cnew_ref[b] = s_ref[end - 2:end, :]

    if tiles_per_seq == 1:
        store_tail()
    else:
        pl.when(j == tiles_per_seq - 1)(store_tail)
        end = SUBLANES + rows_per_seq
        s_ref[6:8, :] = s_ref[end - 2:end, :]

    r2 = jnp.dot(xn, w2_ref[...], preferred_element_type=F32)
    bd = bd_ref[...]
    inv_hd = 1.0 / HEAD_DIM

    def head_scale(t):
        ss = jnp.dot((t * t).astype(BF16), bd, preferred_element_type=F32)
        return lax.rsqrt(ss * inv_hd + EPS)

    for c in range(D_MODEL // MXU_DIM):
        cs = slice(c * MXU_DIM, (c + 1) * MXU_DIM)
        qb = r2[:, cs]
        qs_ref[:, cs] = ((qb * head_scale(qb) * qw_ref[:, cs]) * SCALE).astype(BF16)
    sga_ref[...] = jax.nn.sigmoid(r2[:, D_MODEL:2 * D_MODEL]).astype(BF16)
    k = r2[:, 2 * D_MODEL:2 * D_MODEL + KV_DIM]
    kh_ref[...] = k * head_scale(k) * kw_ref[...]
    v_ref[...] = r2[:, 2 * D_MODEL + KV_DIM:2 * D_MODEL + 2 * KV_DIM]


def _mixer_in(x2d, prefix, nw, w1, w2, cw, bd, qw, kw, *, tile, seq_len):
    n = x2d.shape[0]
    n_seq = n // seq_len
    if tile >= seq_len:
        seqs_per_tile, rows_per_seq, tiles_per_seq = tile // seq_len, seq_len, 1
    else:
        seqs_per_tile, rows_per_seq, tiles_per_seq = 1, tile, seq_len // tile
    assert seqs_per_tile * rows_per_seq == tile and n % tile == 0
    grid = (n_seq // seqs_per_tile, tiles_per_seq)
    row = lambda i, j: (i * tiles_per_seq + j, 0)
    const2 = lambda i, j: (0, 0)
    const3 = lambda i, j: (0, 0, 0)
    seq3 = lambda i, j: (i, 0, 0)
    resident = dict(pipeline_mode=pl.Buffered(1))
    kern = functools.partial(_mixer_in_kernel, seqs_per_tile=seqs_per_tile,
                             rows_per_seq=rows_per_seq, tiles_per_seq=tiles_per_seq)
    return pl.pallas_call(
        kern,
        grid=grid,
        in_specs=[
            pl.BlockSpec((tile, D_MODEL), row),
            pl.BlockSpec((seqs_per_tile, CONV_W - 1, D_CONV), seq3),
            pl.BlockSpec((1, D_MODEL), const2),
            pl.BlockSpec(w1.shape, const3, **resident),
            pl.BlockSpec(w2.shape, const2, **resident),
            pl.BlockSpec(cw.shape, const2),
            pl.BlockSpec(bd.shape, const2),
            pl.BlockSpec(qw.shape, const2),
            pl.BlockSpec(kw.shape, const2),
        ],
        out_specs=[
            pl.BlockSpec((tile, D_CONV), row),
            pl.BlockSpec((tile, D_MODEL), row),
            pl.BlockSpec((tile, D_MODEL), row),
            pl.BlockSpec((tile, KV_DIM), row),
            pl.BlockSpec((tile, KV_DIM), row),
            pl.BlockSpec((seqs_per_tile, CONV_W - 1, D_CONV), seq3),
        ],
        out_shape=[
            jax.ShapeDtypeStruct((n, D_CONV), BF16),
            jax.ShapeDtypeStruct((n, D_MODEL), BF16),
            jax.ShapeDtypeStruct((n, D_MODEL), BF16),
            jax.ShapeDtypeStruct((n, KV_DIM), F32),
            jax.ShapeDtypeStruct((n, KV_DIM), F32),
            jax.ShapeDtypeStruct((n_seq, CONV_W - 1, D_CONV), F32),
        ],
        scratch_shapes=[pltpu.VMEM((seqs_per_tile * (rows_per_seq + SUBLANES), D_CONV), F32)],
        compiler_params=pltpu.CompilerParams(
            dimension_semantics=("arbitrary", "arbitrary"),
            vmem_limit_bytes=VMEM_LIMIT_BYTES),
        name="mixer_in",
    )(x2d, prefix, nw, w1, w2, cw, bd, qw, kw)


def _attention_kernel(q_ref, kp_ref, kc_ref, vp_ref, vc_ref, sink_ref, ycg_ref, sga_ref, o_ref,
                      *, groups, n_new):
    j = pl.program_id(1)
    wide = GROUP * ATT_TOK

    def key_block(ref_prev, ref_cur, i):
        return ref_prev[...] if i == 0 else ref_cur[(i - 1) * ATT_TOK:i * ATT_TOK, :]

    kb = [key_block(kp_ref, kc_ref, i).astype(BF16) for i in range(groups + 1)]
    vt = [key_block(vp_ref, vc_ref, i).T.astype(BF16) for i in range(groups + 1)]
    ones = jnp.ones((2 * SUBLANES, ATT_KEYS), BF16)
    zeros = jnp.zeros((HEAD_DIM, wide), BF16)
    lane_tok = lax.broadcasted_iota(jnp.int32, (CHUNK, wide), 1) % ATT_TOK
    first_half = lane_tok < CHUNK
    has_prev = j > 0

    def scores(g, kv, qt):
        qcat = jnp.concatenate(
            [qt[2 * kv + i // 2][(i % 2) * HEAD_DIM:(i % 2 + 1) * HEAD_DIM] for i in range(GROUP)],
            axis=1)
        rhs = jnp.concatenate([qcat, zeros] if kv % 2 == 0 else [zeros, qcat], axis=0)
        slab = slice((kv // 2) * LANES, (kv // 2 + 1) * LANES)
        kcat = jnp.concatenate([kb[g][:, slab], kb[g + 1][:, slab]], axis=0)
        return jnp.dot(kcat, rhs, preferred_element_type=F32)

    def masked(g, s):
        blk = [s[i * CHUNK:(i + 1) * CHUNK] for i in range(ATT_KEYS // CHUNK)]
        if n_new is None:
            if g == 0:
                blk[0] = jnp.where(first_half & has_prev, blk[0], NEG_INF)
                blk[1] = jnp.where(has_prev, blk[1], NEG_INF)
            else:
                blk[0] = jnp.where(first_half, blk[0], NEG_INF)
            blk[3] = jnp.where(first_half, NEG_INF, blk[3])
        else:
            row = lax.broadcasted_iota(jnp.int32, (CHUNK, wide), 0)
            blk[2] = jnp.where(row < n_new, blk[2], NEG_INF)
            blk[3] = jnp.full_like(blk[3], NEG_INF)
        return jnp.concatenate(blk, axis=0)

    def softmax_pv(g, kv, s):
        s = masked(g, s)
        sink = sink_ref[kv:kv + 1, :]
        m = jnp.maximum(jnp.max(s, axis=0, keepdims=True), sink)
        e = jnp.exp(s - m).astype(BF16)
        vrows = slice(kv * HEAD_DIM, (kv + 1) * HEAD_DIM)
        lhs = jnp.concatenate([vt[g][vrows], vt[g + 1][vrows]], axis=1)
        oe = jnp.dot(jnp.concatenate([lhs, ones], axis=0), e, preferred_element_type=F32)
        den = oe[HEAD_DIM:HEAD_DIM + 1] + jnp.exp(sink - m)
        return oe[:HEAD_DIM] * (1.0 / den)

    def q_transposed(g):
        rows = slice(g * ATT_TOK, (g + 1) * ATT_TOK)
        return [q_ref[rows, p * LANES:(p + 1) * LANES].astype(F32).T.astype(BF16)
                for p in range(D_MODEL // LANES)]

    def merge(g, outs):
        rows = slice(g * ATT_TOK, (g + 1) * ATT_TOK)
        for p in range(D_MODEL // LANES):
            cs = slice(p * LANES, (p + 1) * LANES)
            pair = []
            for head in (2 * p, 2 * p + 1):
                i = head % GROUP
                pair.append(outs[head // GROUP][:, i * ATT_TOK:(i + 1) * ATT_TOK])
            a = jnp.concatenate(pair, axis=0).T
            o_ref[rows, cs] = (ycg_ref[rows, cs].astype(F32)
                               + sga_ref[rows, cs].astype(F32) * a).astype(BF16)

    units = [(g, kv) for g in range(groups) for kv in range(N_KV_HEADS)]
    qts = {0: q_transposed(0)}
    pending = scores(0, 0, qts[0])
    outs = []
    for n, (g, kv) in enumerate(units):
        s = pending
        if n + 1 < len(units):
            g2, kv2 = units[n + 1]
            if g2 not in qts:
                qts[g2] = q_transposed(g2)
            pending = scores(g2, kv2, qts[g2])
        outs.append(softmax_pv(g, kv, s))
        if kv == N_KV_HEADS - 1:
            merge(g, outs)
            outs = []
            del qts[g]


def _attention(qs, k_prev_arr, k_cur_arr, v_prev_arr, v_cur_arr, sink_w, ycg, sga,
               *, n_seq, tile, tiles_per_seq, prev_map, n_new):
    n = qs.shape[0]
    groups = tile // ATT_TOK
    cur = lambda i, j: (i * tiles_per_seq + j, 0)
    const2 = lambda i, j: (0, 0)
    kern = functools.partial(_attention_kernel, groups=groups, n_new=n_new)
    return pl.pallas_call(
        kern,
        grid=(n_seq, tiles_per_seq),
        in_specs=[
            pl.BlockSpec((tile, D_MODEL), cur),
            pl.BlockSpec((WINDOW, KV_DIM), prev_map),
            pl.BlockSpec((tile, KV_DIM), cur),
            pl.BlockSpec((WINDOW, KV_DIM), prev_map),
            pl.BlockSpec((tile, KV_DIM), cur),
            pl.BlockSpec(sink_w.shape, const2),
            pl.BlockSpec((tile, D_MODEL), cur),
            pl.BlockSpec((tile, D_MODEL), cur),
        ],
        out_specs=pl.BlockSpec((tile, D_MODEL), cur),
        out_shape=jax.ShapeDtypeStruct((n, D_MODEL), BF16),
        compiler_params=pltpu.CompilerParams(
            dimension_semantics=("arbitrary", "arbitrary"),
            vmem_limit_bytes=VMEM_LIMIT_BYTES),
        name="attention",
    )(qs, k_prev_arr, k_cur_arr, v_prev_arr, v_cur_arr, sink_w, ycg, sga)


def _out_ffn_kernel(m_ref, x_ref, wo_ref, nw_ref, wg_ref, wu_ref, wd_ref, y_ref):
    x1 = x_ref[...] + jnp.dot(m_ref[...], wo_ref[...], preferred_element_type=F32)
    xn = (x1 * _rms_scale(x1) * nw_ref[...]).astype(BF16)
    acc = x1
    for lo, hi in FF_BLOCKS:
        g = jnp.dot(xn, wg_ref[:, lo:hi], preferred_element_type=F32)
        u = jnp.dot(xn, wu_ref[:, lo:hi], preferred_element_type=F32)
        a = (jax.nn.silu(g) * u).astype(BF16)
        acc = acc + jnp.dot(a, wd_ref[lo:hi, :], preferred_element_type=F32)
    y_ref[...] = acc


def _out_ffn(merged, x2d, wo, nw, wg, wu, wd, *, tile):
    n = x2d.shape[0]
    row = lambda i: (i, 0)
    const2 = lambda i: (0, 0)
    resident = dict(pipeline_mode=pl.Buffered(1))
    return pl.pallas_call(
        _out_ffn_kernel,
        grid=(n // tile,),
        in_specs=[
            pl.BlockSpec((tile, D_MODEL), row),
            pl.BlockSpec((tile, D_MODEL), row),
            pl.BlockSpec(wo.shape, const2, **resident),
            pl.BlockSpec((1, D_MODEL), const2),
            pl.BlockSpec(wg.shape, const2, **resident),
            pl.BlockSpec(wu.shape, const2, **resident),
            pl.BlockSpec(wd.shape, const2, **resident),
        ],
        out_specs=pl.BlockSpec((tile, D_MODEL), row),
        out_shape=jax.ShapeDtypeStruct((n, D_MODEL), F32),
        compiler_params=pltpu.CompilerParams(
            dimension_semantics=("arbitrary",),
            vmem_limit_bytes=VMEM_LIMIT_BYTES),
        name="out_ffn",
    )(merged, x2d, wo, nw, wg, wu, wd)


def _pad_rows(a, n_seq, rows, to_rows):
    a = a.reshape(n_seq, rows, a.shape[-1])
    a = jnp.pad(a, ((0, 0), (0, to_rows - rows), (0, 0)))
    return a.reshape(n_seq * to_rows, a.shape[-1])


def _layer(xp, xs, state_conv, cache_k, cache_v, norm_mix_w, w_in, conv_w, q_norm_w, k_norm_w,
           sinks, w_out, norm_ffn_w, w_gate, w_up, w_down):
    b_p, t_p, _ = xp.shape
    b_s, t_s, _ = xs.shape

    hs, bs, cs_, qs_, ks_, vs_, gcs, gas = (0, D_CONV, 2 * D_CONV, 3 * D_CONV, 3 * D_CONV + D_MODEL,
                                            3 * D_CONV + D_MODEL + KV_DIM,
                                            3 * D_CONV + D_MODEL + 2 * KV_DIM,
                                            4 * D_CONV + D_MODEL + 2 * KV_DIM)
    w_in_b = w_in.astype(BF16)
    w1 = jnp.stack([
        jnp.concatenate([w_in_b[:, s + cb * COL_BLOCK:s + (cb + 1) * COL_BLOCK] for s in (hs, bs, cs_, gcs)],
                        axis=1)
        for cb in range(N_COL_BLOCKS)])
    w2 = jnp.concatenate([w_in_b[:, qs_:qs_ + D_MODEL], w_in_b[:, gas:gas + D_MODEL],
                          w_in_b[:, ks_:ks_ + KV_DIM], w_in_b[:, vs_:vs_ + KV_DIM]], axis=1)
    head_id = jnp.arange(MXU_DIM) // HEAD_DIM
    bd = (head_id[:, None] == head_id[None, :]).astype(BF16)
    nw = norm_mix_w.reshape(1, D_MODEL)
    qw = jnp.tile(q_norm_w, N_HEADS).reshape(1, D_MODEL)
    kw = jnp.tile(k_norm_w, N_KV_HEADS).reshape(1, KV_DIM)
    sink_w = jnp.broadcast_to(sinks.astype(F32).reshape(N_KV_HEADS, GROUP, 1),
                              (N_KV_HEADS, GROUP, ATT_TOK)).reshape(N_KV_HEADS, GROUP * ATT_TOK)
    wo = w_out.astype(BF16)
    nfw = norm_ffn_w.reshape(1, D_MODEL)
    wg, wu, wd = w_gate.astype(BF16), w_up.astype(BF16), w_down.astype(BF16)

    xp2 = xp.reshape(b_p * t_p, D_MODEL)
    zero_prefix = jnp.zeros((b_p, CONV_W - 1, D_CONV), F32)
    ycg, q, sga, kh, v, cnew_p = _mixer_in(xp2, zero_prefix, nw, w1, w2, conv_w, bd, qw, kw,
                                           tile=512, seq_len=t_p)
    att_tile = 512
    blocks_per_tile = att_tile // WINDOW
    blocks_per_seq = t_p // WINDOW
    merged = _attention(
        q, kh, kh, v, v, sink_w, ycg, sga, n_seq=b_p, tile=att_tile, tiles_per_seq=t_p // att_tile,
        prev_map=lambda i, j: (i * blocks_per_seq + jnp.maximum(j * blocks_per_tile - 1, 0), 0),
        n_new=None)
    yp = _out_ffn(merged, xp2, wo, nfw, wg, wu, wd, tile=512).reshape(b_p, t_p, D_MODEL)
    tail = lambda a: a.reshape(b_p, t_p, KV_DIM)[:, -WINDOW:].reshape(b_p, WINDOW, N_KV_HEADS, HEAD_DIM)
    k_p, v_p = tail(kh), tail(v)

    xs2 = xs.reshape(b_s * t_s, D_MODEL)
    ycg_s, q_s, sga_s, kh_s, v_s, cnew_s = _mixer_in(xs2, state_conv, nw, w1, w2, conv_w, bd, qw, kw,
                                                     tile=b_s * t_s, seq_len=t_s)
    pad = functools.partial(_pad_rows, n_seq=b_s, rows=t_s, to_rows=ATT_TOK)
    cache_len = cache_k.shape[1]
    merged_s = _attention(
        pad(q_s), cache_k.reshape(b_s * cache_len, KV_DIM), pad(kh_s),
        cache_v.reshape(b_s * cache_len, KV_DIM), pad(v_s), sink_w, pad(ycg_s), pad(sga_s),
        n_seq=b_s, tile=ATT_TOK, tiles_per_seq=1, prev_map=lambda i, j: (i, 0), n_new=t_s)
    merged_s = merged_s.reshape(b_s, ATT_TOK, D_MODEL)[:, :t_s].reshape(b_s * t_s, D_MODEL)
    ys = _out_ffn(merged_s, xs2, wo, nfw, wg, wu, wd, tile=b_s * t_s).reshape(b_s, t_s, D_MODEL)
    k_s = kh_s.reshape(b_s, t_s, N_KV_HEADS, HEAD_DIM)
    v_s = v_s.reshape(b_s, t_s, N_KV_HEADS, HEAD_DIM)
    return yp, ys, cnew_p, k_p, v_p, cnew_s, k_s, v_s


def kernel(x_prompt, x_sample, state_conv, cache_k, cache_v, norm_mix_w, w_in, conv_w, q_norm_w, k_norm_w, sinks, w_out, norm_ffn_w, w_gate, w_up, w_down):
    depth = w_in.shape[0]
    assert cache_k.shape[2] == WINDOW, "sample attention assumes a full window of cached rows"
    yp, ys = x_prompt, x_sample
    outs = [[] for _ in range(6)]
    for l in range(depth):
        yp, ys, *rest = _layer(
            yp, ys, state_conv[l], cache_k[l], cache_v[l], norm_mix_w[l], w_in[l], conv_w[l],
            q_norm_w[l], k_norm_w[l], sinks[l], w_out[l], norm_ffn_w[l], w_gate[l], w_up[l], w_down[l])
        for acc, val in zip(outs, rest):
            acc.append(val)
    return (yp, ys) + tuple(jnp.stack(o) for o in outs)
```

```python
import functools

import jax
import jax.numpy as jnp
from jax import lax
from jax.experimental import pallas as pl
from jax.experimental.pallas import tpu as pltpu

D_MODEL = 1024
HEAD_DIM = 64
N_HEADS = 16
N_KV_HEADS = 4
GROUP = N_HEADS // N_KV_HEADS
KV_DIM = N_KV_HEADS * HEAD_DIM
D_CONV = D_MODEL
CONV_W = 3
D_FF = 2816
CHUNK = 64
WINDOW = 128
EPS = 1e-6
NEG_INF = -1e30
SCALE = HEAD_DIM ** -0.5

LANES = 128
SUBLANES = 8
MXU_DIM = 256
VMEM_LIMIT_BYTES = 56 * 1024 * 1024

COL_BLOCK = MXU_DIM
N_COL_BLOCKS = D_CONV // COL_BLOCK
ATT_TOK = 2 * CHUNK
ATT_KEYS = ATT_TOK + WINDOW
FF_BLOCKS = ((0, 1024), (1024, 2048), (2048, D_FF))
W2_SLABS = ((0, D_MODEL + 2 * KV_DIM), (D_MODEL + 2 * KV_DIM, 2 * D_MODEL + 2 * KV_DIM))
LOG2E = 1.4426950408889634

BF16 = jnp.bfloat16
F32 = jnp.float32


def _rms_scale(x):
    return lax.rsqrt(jnp.mean(x * x, axis=-1, keepdims=True) + EPS)


def _mixer_in_kernel(x_ref, pre_ref, nw_ref, w1_ref, w2_ref, cw_ref, bd_ref, qw_ref, kw_ref,
                     ycg_ref, qs_ref, sga_ref, kh_ref, v_ref, cnew_ref, s_ref,
                     *, seqs_per_tile, rows_per_seq, tiles_per_seq):
    j = pl.program_id(1)
    slot = rows_per_seq + SUBLANES
    x = x_ref[...]
    xn = (x * _rms_scale(x) * nw_ref[...]).astype(BF16)

    def load_prefix():
        for b in range(seqs_per_tile):
            s_ref[b * slot + 6:b * slot + 8, :] = pre_ref[b]

    if tiles_per_seq == 1:
        load_prefix()
    else:
        pl.when(j == 0)(load_prefix)

    def project(slab):
        if slab < N_COL_BLOCKS:
            return jnp.dot(xn, w1_ref[slab], preferred_element_type=F32)
        lo, hi = W2_SLABS[slab - N_COL_BLOCKS]
        return jnp.dot(xn, w2_ref[:, lo:hi], preferred_element_type=F32)

    pending = project(0)
    for cb in range(N_COL_BLOCKS):
        cs = slice(cb * COL_BLOCK, (cb + 1) * COL_BLOCK)
        r = pending
        pending = project(cb + 1)
        h = r[:, 0 * COL_BLOCK:1 * COL_BLOCK]
        gb = r[:, 1 * COL_BLOCK:2 * COL_BLOCK]
        gc = r[:, 2 * COL_BLOCK:3 * COL_BLOCK]
        gcv = r[:, 3 * COL_BLOCK:4 * COL_BLOCK]
        u = gc * h
        for b in range(seqs_per_tile):
            rs = slice(b * rows_per_seq, (b + 1) * rows_per_seq)
            base = b * slot + SUBLANES
            ub = u[rs]
            s_ref[base:base + rows_per_seq, cs] = ub
            um1 = s_ref[base - 1:base - 1 + rows_per_seq, cs]
            um2 = s_ref[base - 2:base - 2 + rows_per_seq, cs]
            y = cw_ref[0:1, cs] * um2 + cw_ref[1:2, cs] * um1 + cw_ref[2:3, cs] * ub
            ycg_ref[rs, cs] = (jax.nn.sigmoid(gcv[rs]) * (gb[rs] * y)).astype(BF16)

    def store_tail():
        for b in range(seqs_per_tile):
            end = b * slot + SUBLANES + rows_per_seq
            cnew_ref[b] = s_ref[end - 2:end, :]

    if tiles_per_seq == 1:
        store_tail()
    else:
        pl.when(j == tiles_per_seq - 1)(store_tail)
        end = SUBLANES + rows_per_seq
        s_ref[6:8, :] = s_ref[end - 2:end, :]

    qkv = pending
    pending = project(N_COL_BLOCKS + 1)
    bd = bd_ref[...]

    def head_scale(t):
        ms = jnp.dot((t * t).astype(BF16), bd, preferred_element_type=F32)
        return lax.rsqrt(ms + EPS)

    for c in range(D_MODEL // MXU_DIM):
        cs = slice(c * MXU_DIM, (c + 1) * MXU_DIM)
        qb = qkv[:, cs]
        qs_ref[:, cs] = (qb * head_scale(qb) * qw_ref[:, cs]).astype(BF16)
    k = qkv[:, D_MODEL:D_MODEL + KV_DIM]
    kh_ref[...] = k * head_scale(k) * kw_ref[...]
    v_ref[...] = qkv[:, D_MODEL + KV_DIM:D_MODEL + 2 * KV_DIM]
    sga_ref[...] = jax.nn.sigmoid(pending).astype(BF16)


def _mixer_in(x2d, prefix, nw, w1, w2, cw, bd, qw, kw, *, tile, seq_len):
    n = x2d.shape[0]
    n_seq = n // seq_len
    if tile >= seq_len:
        seqs_per_tile, rows_per_seq, tiles_per_seq = tile // seq_len, seq_len, 1
    else:
        seqs_per_tile, rows_per_seq, tiles_per_seq = 1, tile, seq_len // tile
    assert seqs_per_tile * rows_per_seq == tile and n % tile == 0
    grid = (n_seq // seqs_per_tile, tiles_per_seq)
    row = lambda i, j: (i * tiles_per_seq + j, 0)
    const2 = lambda i, j: (0, 0)
    const3 = lambda i, j: (0, 0, 0)
    seq3 = lambda i, j: (i, 0, 0)
    resident = dict(pipeline_mode=pl.Buffered(1))
    kern = functools.partial(_mixer_in_kernel, seqs_per_tile=seqs_per_tile,
                             rows_per_seq=rows_per_seq, tiles_per_seq=tiles_per_seq)
    return pl.pallas_call(
        kern,
        grid=grid,
        in_specs=[
            pl.BlockSpec((tile, D_MODEL), row),
            pl.BlockSpec((seqs_per_tile, CONV_W - 1, D_CONV), seq3),
            pl.BlockSpec((1, D_MODEL), const2),
            pl.BlockSpec(w1.shape, const3, **resident),
            pl.BlockSpec(w2.shape, const2, **resident),
            pl.BlockSpec(cw.shape, const2),
            pl.BlockSpec(bd.shape, const2),
            pl.BlockSpec(qw.shape, const2),
            pl.BlockSpec(kw.shape, const2),
        ],
        out_specs=[
            pl.BlockSpec((tile, D_CONV), row),
            pl.BlockSpec((tile, D_MODEL), row),
            pl.BlockSpec((tile, D_MODEL), row),
            pl.BlockSpec((tile, KV_DIM), row),
            pl.BlockSpec((tile, KV_DIM), row),
            pl.BlockSpec((seqs_per_tile, CONV_W - 1, D_CONV), seq3),
        ],
        out_shape=[
            jax.ShapeDtypeStruct((n, D_CONV), BF16),
            jax.ShapeDtypeStruct((n, D_MODEL), BF16),
            jax.ShapeDtypeStruct((n, D_MODEL), BF16),
            jax.ShapeDtypeStruct((n, KV_DIM), F32),
            jax.ShapeDtypeStruct((n, KV_DIM), F32),
            jax.ShapeDtypeStruct((n_seq, CONV_W - 1, D_CONV), F32),
        ],
        scratch_shapes=[pltpu.VMEM((seqs_per_tile * (rows_per_seq + SUBLANES), D_CONV), F32)],
        compiler_params=pltpu.CompilerParams(
            dimension_semantics=("arbitrary", "arbitrary"),
            vmem_limit_bytes=VMEM_LIMIT_BYTES),
        name="mixer_in",
    )(x2d, prefix, nw, w1, w2, cw, bd, qw, kw)


def _attention_kernel(q_ref, kp_ref, kc_ref, vp_ref, vc_ref, sink_ref, ycg_ref, sga_ref, o_ref,
                      *, groups, n_new):
    j = pl.program_id(1)
    wide = GROUP * ATT_TOK

    def key_block(ref_prev, ref_cur, i):
        return ref_prev[...] if i == 0 else ref_cur[(i - 1) * ATT_TOK:i * ATT_TOK, :]

    kb = [key_block(kp_ref, kc_ref, i).astype(BF16) for i in range(groups + 1)]
    vt = [key_block(vp_ref, vc_ref, i).T.astype(BF16) for i in range(groups + 1)]
    ones = jnp.ones((2 * SUBLANES, ATT_KEYS), BF16)
    zeros = jnp.zeros((HEAD_DIM, wide), BF16)
    first_half = lax.broadcasted_iota(jnp.int32, (CHUNK, ATT_TOK), 1) < CHUNK
    has_prev = j > 0

    def scores(g, kv, qt):
        qcat = jnp.concatenate(
            [qt[2 * kv + i // 2][(i % 2) * HEAD_DIM:(i % 2 + 1) * HEAD_DIM] for i in range(GROUP)],
            axis=1)
        rhs = jnp.concatenate([qcat, zeros] if kv % 2 == 0 else [zeros, qcat], axis=0)
        slab = slice((kv // 2) * LANES, (kv // 2 + 1) * LANES)
        kcat = jnp.concatenate([kb[g][:, slab], kb[g + 1][:, slab]], axis=0)
        return jnp.dot(kcat, rhs, preferred_element_type=F32)

    def masked(g, s):
        blk = [s[i * CHUNK:(i + 1) * CHUNK] for i in range(ATT_KEYS // CHUNK)]
        if n_new is None:
            if g == 0:
                blk[0] = jnp.where(first_half & has_prev, blk[0], NEG_INF)
                blk[1] = jnp.where(has_prev, blk[1], NEG_INF)
            else:
                blk[0] = jnp.where(first_half, blk[0], NEG_INF)
            blk[3] = jnp.where(first_half, NEG_INF, blk[3])
        else:
            row = lax.broadcasted_iota(jnp.int32, (CHUNK, ATT_TOK), 0)
            blk[2] = jnp.where(row < n_new, blk[2], NEG_INF)
            blk[3] = jnp.full_like(blk[3], NEG_INF)
        return jnp.concatenate(blk, axis=0)

    def softmax_pv(g, kv, s):
        es, sink_terms = [], []
        for i in range(GROUP):
            lanes = slice(i * ATT_TOK, (i + 1) * ATT_TOK)
            si = masked(g, s[:, lanes])
            sink = sink_ref[kv:kv + 1, lanes]
            m = jnp.maximum(jnp.max(si, axis=0, keepdims=True), sink)
            es.append(jnp.exp2(si - m).astype(BF16))
            sink_terms.append(jnp.exp2(sink - m))
        e = jnp.concatenate(es, axis=1)
        vrows = slice(kv * HEAD_DIM, (kv + 1) * HEAD_DIM)
        lhs = jnp.concatenate([vt[g][vrows], vt[g + 1][vrows]], axis=1)
        oe = jnp.dot(jnp.concatenate([lhs, ones], axis=0), e, preferred_element_type=F32)
        den = oe[HEAD_DIM:HEAD_DIM + 1] + jnp.concatenate(sink_terms, axis=1)
        return oe[:HEAD_DIM] * (1.0 / den)

    def q_transposed(g):
        rows = slice(g * ATT_TOK, (g + 1) * ATT_TOK)
        return [q_ref[rows, p * LANES:(p + 1) * LANES].astype(F32).T.astype(BF16)
                for p in range(D_MODEL // LANES)]

    def merge(g, outs):
        rows = slice(g * ATT_TOK, (g + 1) * ATT_TOK)
        for p in range(D_MODEL // LANES):
            cs = slice(p * LANES, (p + 1) * LANES)
            pair = []
            for head in (2 * p, 2 * p + 1):
                i = head % GROUP
                pair.append(outs[head // GROUP][:, i * ATT_TOK:(i + 1) * ATT_TOK])
            a = jnp.concatenate(pair, axis=0).T
            o_ref[rows, cs] = (ycg_ref[rows, cs].astype(F32)
                               + sga_ref[rows, cs].astype(F32) * a).astype(BF16)

    units = [(g, kv) for g in range(groups) for kv in range(N_KV_HEADS)]
    qts = {0: q_transposed(0)}
    pending = scores(0, 0, qts[0])
    outs = []
    for n, (g, kv) in enumerate(units):
        s = pending
        if n + 1 < len(units):
            g2, kv2 = units[n + 1]
            if g2 not in qts:
                qts[g2] = q_transposed(g2)
            pending = scores(g2, kv2, qts[g2])
        outs.append(softmax_pv(g, kv, s))
        if kv == N_KV_HEADS - 1:
            merge(g, outs)
            outs = []
            del qts[g]


def _attention(qs, k_prev_arr, k_cur_arr, v_prev_arr, v_cur_arr, sink_w, ycg, sga,
               *, n_seq, tile, tiles_per_seq, prev_map, n_new):
    n = qs.shape[0]
    groups = tile // ATT_TOK
    cur = lambda i, j: (i * tiles_per_seq + j, 0)
    const2 = lambda i, j: (0, 0)
    kern = functools.partial(_attention_kernel, groups=groups, n_new=n_new)
    return pl.pallas_call(
        kern,
        grid=(n_seq, tiles_per_seq),
        in_specs=[
            pl.BlockSpec((tile, D_MODEL), cur),
            pl.BlockSpec((WINDOW, KV_DIM), prev_map),
            pl.BlockSpec((tile, KV_DIM), cur),
            pl.BlockSpec((WINDOW, KV_DIM), prev_map),
            pl.BlockSpec((tile, KV_DIM), cur),
            pl.BlockSpec(sink_w.shape, const2),
            pl.BlockSpec((tile, D_MODEL), cur),
            pl.BlockSpec((tile, D_MODEL), cur),
        ],
        out_specs=pl.BlockSpec((tile, D_MODEL), cur),
        out_shape=jax.ShapeDtypeStruct((n, D_MODEL), BF16),
        compiler_params=pltpu.CompilerParams(
            dimension_semantics=("arbitrary", "arbitrary"),
            vmem_limit_bytes=VMEM_LIMIT_BYTES),
        name="attention",
    )(qs, k_prev_arr, k_cur_arr, v_prev_arr, v_cur_arr, sink_w, ycg, sga)


def _out_ffn_kernel(m_ref, x_ref, wo_ref, nw_ref, wg_ref, wu_ref, wd_ref, y_ref):
    x1 = x_ref[...] + jnp.dot(m_ref[...], wo_ref[...], preferred_element_type=F32)
    xn = (x1 * _rms_scale(x1) * nw_ref[...]).astype(BF16)
    acc = x1
    for lo, hi in FF_BLOCKS:
        g = jnp.dot(xn, wg_ref[:, lo:hi], preferred_element_type=F32)
        u = jnp.dot(xn, wu_ref[:, lo:hi], preferred_element_type=F32)
        a = (jax.nn.silu(g) * u).astype(BF16)
        acc = acc + jnp.dot(a, wd_ref[lo:hi, :], preferred_element_type=F32)
    y_ref[...] = acc


def _out_ffn(merged, x2d, wo, nw, wg, wu, wd, *, tile):
    n = x2d.shape[0]
    row = lambda i: (i, 0)
    const2 = lambda i: (0, 0)
    resident = dict(pipeline_mode=pl.Buffered(1))
    return pl.pallas_call(
        _out_ffn_kernel,
        grid=(n // tile,),
        in_specs=[
            pl.BlockSpec((tile, D_MODEL), row),
            pl.BlockSpec((tile, D_MODEL), row),
            pl.BlockSpec(wo.shape, const2, **resident),
            pl.BlockSpec((1, D_MODEL), const2),
            pl.BlockSpec(wg.shape, const2, **resident),
            pl.BlockSpec(wu.shape, const2, **resident),
            pl.BlockSpec(wd.shape, const2, **resident),
        ],
        out_specs=pl.BlockSpec((tile, D_MODEL), row),
        out_shape=jax.ShapeDtypeStruct((n, D_MODEL), F32),
        compiler_params=pltpu.CompilerParams(
            dimension_semantics=("arbitrary",),
            vmem_limit_bytes=VMEM_LIMIT_BYTES),
        name="out_ffn",
    )(merged, x2d, wo, nw, wg, wu, wd)


def _pad_rows(a, n_seq, rows, to_rows):
    a = a.reshape(n_seq, rows, a.shape[-1])
    a = jnp.pad(a, ((0, 0), (0, to_rows - rows), (0, 0)))
    return a.reshape(n_seq * to_rows, a.shape[-1])


def _layer(xp, xs, state_conv, cache_k, cache_v, norm_mix_w, w_in, conv_w, q_norm_w, k_norm_w,
           sinks, w_out, norm_ffn_w, w_gate, w_up, w_down):
    b_p, t_p, _ = xp.shape
    b_s, t_s, _ = xs.shape

    hs, bs, cs_, qs_, ks_, vs_, gcs, gas = (0, D_CONV, 2 * D_CONV, 3 * D_CONV, 3 * D_CONV + D_MODEL,
                                            3 * D_CONV + D_MODEL + KV_DIM,
                                            3 * D_CONV + D_MODEL + 2 * KV_DIM,
                                            4 * D_CONV + D_MODEL + 2 * KV_DIM)
    w_in_b = w_in.astype(BF16)
    w1 = jnp.stack([
        jnp.concatenate([w_in_b[:, s + cb * COL_BLOCK:s + (cb + 1) * COL_BLOCK] for s in (hs, bs, cs_, gcs)],
                        axis=1)
        for cb in range(N_COL_BLOCKS)])
    w2 = jnp.concatenate([w_in_b[:, qs_:qs_ + D_MODEL], w_in_b[:, ks_:ks_ + KV_DIM],
                          w_in_b[:, vs_:vs_ + KV_DIM], w_in_b[:, gas:gas + D_MODEL]], axis=1)
    head_id = jnp.arange(MXU_DIM) // HEAD_DIM
    bd = ((head_id[:, None] == head_id[None, :]) * (1.0 / HEAD_DIM)).astype(BF16)
    nw = norm_mix_w.reshape(1, D_MODEL)
    qw = (jnp.tile(q_norm_w, N_HEADS) * (SCALE * LOG2E)).reshape(1, D_MODEL)
    kw = jnp.tile(k_norm_w, N_KV_HEADS).reshape(1, KV_DIM)
    sink_w = jnp.broadcast_to((sinks.astype(F32) * LOG2E).reshape(N_KV_HEADS, GROUP, 1),
                              (N_KV_HEADS, GROUP, ATT_TOK)).reshape(N_KV_HEADS, GROUP * ATT_TOK)
    wo = w_out.astype(BF16)
    nfw = norm_ffn_w.reshape(1, D_MODEL)
    wg, wu, wd = w_gate.astype(BF16), w_up.astype(BF16), w_down.astype(BF16)

    xp2 = xp.reshape(b_p * t_p, D_MODEL)
    zero_prefix = jnp.zeros((b_p, CONV_W - 1, D_CONV), F32)
    ycg, q, sga, kh, v, cnew_p = _mixer_in(xp2, zero_prefix, nw, w1, w2, conv_w, bd, qw, kw,
                                           tile=512, seq_len=t_p)
    att_tile = 512
    blocks_per_tile = att_tile // WINDOW
    blocks_per_seq = t_p // WINDOW
    merged = _attention(
        q, kh, kh, v, v, sink_w, ycg, sga, n_seq=b_p, tile=att_tile, tiles_per_seq=t_p // att_tile,
        prev_map=lambda i, j: (i * blocks_per_seq + jnp.maximum(j * blocks_per_tile - 1, 0), 0),
        n_new=None)
    yp = _out_ffn(merged, xp2, wo, nfw, wg, wu, wd, tile=512).reshape(b_p, t_p, D_MODEL)
    tail = lambda a: a.reshape(b_p, t_p, KV_DIM)[:, -WINDOW:].reshape(b_p, WINDOW, N_KV_HEADS, HEAD_DIM)
    k_p, v_p = tail(kh), tail(v)

    xs2 = xs.reshape(b_s * t_s, D_MODEL)
    ycg_s, q_s, sga_s, kh_s, v_s, cnew_s = _mixer_in(xs2, state_conv, nw, w1, w2, conv_w, bd, qw, kw,
                                                     tile=b_s * t_s, seq_len=t_s)
    pad = functools.partial(_pad_rows, n_seq=b_s, rows=t_s, to_rows=ATT_TOK)
    cache_len = cache_k.shape[1]
    merged_s = _attention(
        pad(q_s), cache_k.reshape(b_s * cache_len, KV_DIM), pad(kh_s),
        cache_v.reshape(b_s * cache_len, KV_DIM), pad(v_s), sink_w, pad(ycg_s), pad(sga_s),
        n_seq=b_s, tile=ATT_TOK, tiles_per_seq=1, prev_map=lambda i, j: (i, 0), n_new=t_s)
    merged_s = merged_s.reshape(b_s, ATT_TOK, D_MODEL)[:, :t_s].reshape(b_s * t_s, D_MODEL)
    ys = _out_ffn(merged_s, xs2, wo, nfw, wg, wu, wd, tile=b_s * t_s).reshape(b_s, t_s, D_MODEL)
    k_s = kh_s.reshape(b_s, t_s, N_KV_HEADS, HEAD_DIM)
    v_s = v_s.reshape(b_s, t_s, N_KV_HEADS, HEAD_DIM)
    return yp, ys, cnew_p, k_p, v_p, cnew_s, k_s, v_s


def kernel(x_prompt, x_sample, state_conv, cache_k, cache_v, norm_mix_w, w_in, conv_w, q_norm_w, k_norm_w, sinks, w_out, norm_ffn_w, w_gate, w_up, w_down):
    depth = w_in.shape[0]
    assert cache_k.shape[2] == WINDOW, "sample attention assumes a full window of cached rows"
    yp, ys = x_prompt, x_sample
    outs = [[] for _ in range(6)]
    for l in range(depth):
        yp, ys, *rest = _layer(
            yp, ys, state_conv[l], cache_k[l], cache_v[l], norm_mix_w[l], w_in[l], conv_w[l],
            q_norm_w[l], k_norm_w[l], sinks[l], w_out[l], norm_ffn_w[l], w_gate[l], w_up[l], w_down[l])
        for acc, val in zip(outs, rest):
            acc.append(val)
    return (yp, ys) + tuple(jnp.stack(o) for o in outs)
```

```python
import functools

import jax
import jax.numpy as jnp
from jax import lax
from jax.experimental import pallas as pl
from jax.experimental.pallas import tpu as pltpu

D_MODEL = 1024
HEAD_DIM = 64
N_HEADS = 16
N_KV_HEADS = 4
GROUP = N_HEADS // N_KV_HEADS
KV_DIM = N_KV_HEADS * HEAD_DIM
D_CONV = D_MODEL
CONV_W = 3
D_FF = 2816
CHUNK = 64
WINDOW = 128
EPS = 1e-6
NEG_INF = -1e30
SCALE = HEAD_DIM ** -0.5

LANES = 128
SUBLANES = 8
MXU_DIM = 256
VMEM_LIMIT_BYTES = 56 * 1024 * 1024

COL_BLOCK = MXU_DIM
N_COL_BLOCKS = D_CONV // COL_BLOCK
ATT_TOK = 2 * CHUNK
ATT_KEYS = ATT_TOK + WINDOW
FF_BLOCKS = ((0, 1024), (1024, 2048), (2048, D_FF))
_QKV = D_MODEL + 2 * KV_DIM
W2_SLABS = ((0, _QKV), (_QKV, _QKV + 3 * MXU_DIM), (_QKV + 3 * MXU_DIM, _QKV + D_MODEL))
LOG2E = 1.4426950408889634

BF16 = jnp.bfloat16
F32 = jnp.float32


def _rms_scale(x):
    return lax.rsqrt(jnp.mean(x * x, axis=-1, keepdims=True) + EPS)


def _mixer_in_kernel(x_ref, pre_ref, nw_ref, w1_ref, w2_ref, cw_ref, bd_ref, qw_ref, kw_ref,
                     ycg_ref, qs_ref, sga_ref, kh_ref, v_ref, cnew_ref, s_ref,
                     *, seqs_per_tile, rows_per_seq, tiles_per_seq):
    j = pl.program_id(1)
    slot = rows_per_seq + SUBLANES

    def load_prefix():
        for b in range(seqs_per_tile):
            s_ref[b * slot + 6:b * slot + 8, :] = pre_ref[b]

    if tiles_per_seq == 1:
        load_prefix()
    else:
        pl.when(j == 0)(load_prefix)

    x = x_ref[...]
    xn = (x * _rms_scale(x) * nw_ref[...]).astype(BF16)

    def project(slab):
        if slab < N_COL_BLOCKS:
            return jnp.dot(xn, w1_ref[slab], preferred_element_type=F32)
        lo, hi = W2_SLABS[slab - N_COL_BLOCKS]
        return jnp.dot(xn, w2_ref[:, lo:hi], preferred_element_type=F32)

    qkv = project(N_COL_BLOCKS)
    pending = project(0)
    bd = bd_ref[...]

    def head_scale(t):
        ms = jnp.dot((t * t).astype(BF16), bd, preferred_element_type=F32)
        return lax.rsqrt(ms + EPS)

    for c in range(D_MODEL // MXU_DIM):
        cs = slice(c * MXU_DIM, (c + 1) * MXU_DIM)
        qb = qkv[:, cs]
        qs_ref[:, cs] = (qb * head_scale(qb) * qw_ref[:, cs]).astype(BF16)
    k = qkv[:, D_MODEL:D_MODEL + KV_DIM]
    kh_ref[...] = k * head_scale(k) * kw_ref[...]
    v_ref[...] = qkv[:, D_MODEL + KV_DIM:D_MODEL + 2 * KV_DIM]

    for cb in range(N_COL_BLOCKS):
        cs = slice(cb * COL_BLOCK, (cb + 1) * COL_BLOCK)
        r = pending
        pending = project(cb + 1 if cb + 1 < N_COL_BLOCKS else N_COL_BLOCKS + 1)
        h = r[:, 0 * COL_BLOCK:1 * COL_BLOCK]
        gb = r[:, 1 * COL_BLOCK:2 * COL_BLOCK]
        gc = r[:, 2 * COL_BLOCK:3 * COL_BLOCK]
        gcv = r[:, 3 * COL_BLOCK:4 * COL_BLOCK]
        u = gc * h
        for b in range(seqs_per_tile):
            rs = slice(b * rows_per_seq, (b + 1) * rows_per_seq)
            base = b * slot + SUBLANES
            ub = u[rs]
            s_ref[base:base + rows_per_seq, cs] = ub
            um1 = s_ref[base - 1:base - 1 + rows_per_seq, cs]
            um2 = s_ref[base - 2:base - 2 + rows_per_seq, cs]
            y = cw_ref[0:1, cs] * um2 + cw_ref[1:2, cs] * um1 + cw_ref[2:3, cs] * ub
            ycg_ref[rs, cs] = (jax.nn.sigmoid(gcv[rs]) * (gb[rs] * y)).astype(BF16)

    for b in range(seqs_per_tile):
        end = b * slot + SUBLANES + rows_per_seq
        cnew_ref[b] = s_ref[end - 2:end, :]
    if tiles_per_seq > 1:
        end = SUBLANES + rows_per_seq
        s_ref[6:8, :] = s_ref[end - 2:end, :]

    ga_head = pending
    pending = project(N_COL_BLOCKS + 2)
    split = W2_SLABS[1][1] - W2_SLABS[1][0]
    sga_ref[:, :split] = jax.nn.sigmoid(ga_head).astype(BF16)
    sga_ref[:, split:] = jax.nn.sigmoid(pending).astype(BF16)


def _mixer_in(x2d, prefix, nw, w1, w2, cw, bd, qw, kw, *, tile, seq_len):
    n = x2d.shape[0]
    n_seq = n // seq_len
    if tile >= seq_len:
        seqs_per_tile, rows_per_seq, tiles_per_seq = tile // seq_len, seq_len, 1
    else:
        seqs_per_tile, rows_per_seq, tiles_per_seq = 1, tile, seq_len // tile
    assert seqs_per_tile * rows_per_seq == tile and n % tile == 0
    grid = (n_seq // seqs_per_tile, tiles_per_seq)
    row = lambda i, j: (i * tiles_per_seq + j, 0)
    const2 = lambda i, j: (0, 0)
    const3 = lambda i, j: (0, 0, 0)
    seq3 = lambda i, j: (i, 0, 0)
    resident = dict(pipeline_mode=pl.Buffered(1))
    kern = functools.partial(_mixer_in_kernel, seqs_per_tile=seqs_per_tile,
                             rows_per_seq=rows_per_seq, tiles_per_seq=tiles_per_seq)
    return pl.pallas_call(
        kern,
        grid=grid,
        in_specs=[
            pl.BlockSpec((tile, D_MODEL), row),
            pl.BlockSpec((seqs_per_tile, CONV_W - 1, D_CONV), seq3),
            pl.BlockSpec((1, D_MODEL), const2),
            pl.BlockSpec(w1.shape, const3, **resident),
            pl.BlockSpec(w2.shape, const2, **resident),
            pl.BlockSpec(cw.shape, const2),
            pl.BlockSpec(bd.shape, const2),
            pl.BlockSpec(qw.shape, const2),
            pl.BlockSpec(kw.shape, const2),
        ],
        out_specs=[
            pl.BlockSpec((tile, D_CONV), row),
            pl.BlockSpec((tile, D_MODEL), row),
            pl.BlockSpec((tile, D_MODEL), row),
            pl.BlockSpec((tile, KV_DIM), row),
            pl.BlockSpec((tile, KV_DIM), row),
            pl.BlockSpec((seqs_per_tile, CONV_W - 1, D_CONV), seq3),
        ],
        out_shape=[
            jax.ShapeDtypeStruct((n, D_CONV), BF16),
            jax.ShapeDtypeStruct((n, D_MODEL), BF16),
            jax.ShapeDtypeStruct((n, D_MODEL), BF16),
            jax.ShapeDtypeStruct((n, KV_DIM), F32),
            jax.ShapeDtypeStruct((n, KV_DIM), F32),
            jax.ShapeDtypeStruct((n_seq, CONV_W - 1, D_CONV), F32),
        ],
        scratch_shapes=[pltpu.VMEM((seqs_per_tile * (rows_per_seq + SUBLANES), D_CONV), F32)],
        compiler_params=pltpu.CompilerParams(
            dimension_semantics=("arbitrary", "arbitrary"),
            vmem_limit_bytes=VMEM_LIMIT_BYTES),
        name="mixer_in",
    )(x2d, prefix, nw, w1, w2, cw, bd, qw, kw)


def _attention_kernel(q_ref, kp_ref, kc_ref, vp_ref, vc_ref, sink_ref, ycg_ref, sga_ref, o_ref,
                      *, groups, n_new):
    j = pl.program_id(1)
    wide = GROUP * ATT_TOK

    own = ATT_TOK if n_new is None else n_new

    def own_rows(ref, g, cols=slice(None)):
        blk = ref[g * own:(g + 1) * own, cols]
        if own == ATT_TOK:
            return blk
        return jnp.concatenate([blk, jnp.zeros((ATT_TOK - own, blk.shape[1]), blk.dtype)], axis=0)

    def key_block(ref_prev, ref_cur, i):
        return ref_prev[...] if i == 0 else own_rows(ref_cur, i - 1)

    kb = [key_block(kp_ref, kc_ref, i).astype(BF16) for i in range(groups + 1)]
    vt = [key_block(vp_ref, vc_ref, i).T.astype(BF16) for i in range(groups + 1)]
    ones = jnp.ones((2 * SUBLANES, ATT_KEYS), BF16)
    zeros = jnp.zeros((HEAD_DIM, wide), BF16)
    first_half = lax.broadcasted_iota(jnp.int32, (CHUNK, ATT_TOK), 1) < CHUNK
    has_prev = j > 0

    def scores(g, kv, qt):
        qcat = jnp.concatenate(
            [qt[2 * kv + i // 2][(i % 2) * HEAD_DIM:(i % 2 + 1) * HEAD_DIM] for i in range(GROUP)],
            axis=1)
        rhs = jnp.concatenate([qcat, zeros] if kv % 2 == 0 else [zeros, qcat], axis=0)
        slab = slice((kv // 2) * LANES, (kv // 2 + 1) * LANES)
        kcat = jnp.concatenate([kb[g][:, slab], kb[g + 1][:, slab]], axis=0)
        return jnp.dot(kcat, rhs, preferred_element_type=F32)

    def masked(g, s):
        blk = [s[i * CHUNK:(i + 1) * CHUNK] for i in range(ATT_KEYS // CHUNK)]
        if n_new is None:
            if g == 0:
                blk[0] = jnp.where(first_half & has_prev, blk[0], NEG_INF)
                blk[1] = jnp.where(has_prev, blk[1], NEG_INF)
            else:
                blk[0] = jnp.where(first_half, blk[0], NEG_INF)
            blk[3] = jnp.where(first_half, NEG_INF, blk[3])
        else:
            row = lax.broadcasted_iota(jnp.int32, (CHUNK, ATT_TOK), 0)
            blk[2] = jnp.where(row < n_new, blk[2], NEG_INF)
            blk[3] = jnp.full_like(blk[3], NEG_INF)
        return jnp.concatenate(blk, axis=0)

    def softmax_pv(g, kv, s):
        es, sink_terms = [], []
        for i in range(GROUP):
            lanes = slice(i * ATT_TOK, (i + 1) * ATT_TOK)
            si = masked(g, s[:, lanes])
            sink = sink_ref[kv:kv + 1, lanes]
            m = jnp.maximum(jnp.max(si, axis=0, keepdims=True), sink)
            es.append(jnp.exp2(si - m).astype(BF16))
            sink_terms.append(jnp.exp2(sink - m))
        e = jnp.concatenate(es, axis=1)
        vrows = slice(kv * HEAD_DIM, (kv + 1) * HEAD_DIM)
        lhs = jnp.concatenate([vt[g][vrows], vt[g + 1][vrows]], axis=1)
        oe = jnp.dot(jnp.concatenate([lhs, ones], axis=0), e, preferred_element_type=F32)
        den = oe[HEAD_DIM:HEAD_DIM + 1] + jnp.concatenate(sink_terms, axis=1)
        return oe[:HEAD_DIM] * (1.0 / den)

    def q_transposed(g):
        return [own_rows(q_ref, g, slice(p * LANES, (p + 1) * LANES)).astype(F32).T.astype(BF16)
                for p in range(D_MODEL // LANES)]

    def merge(g, outs):
        rows = slice(g * own, (g + 1) * own)
        for p in range(D_MODEL // LANES):
            cs = slice(p * LANES, (p + 1) * LANES)
            pair = []
            for head in (2 * p, 2 * p + 1):
                i = head % GROUP
                pair.append(outs[head // GROUP][:, i * ATT_TOK:(i + 1) * ATT_TOK])
            a = jnp.concatenate(pair, axis=0).T[:own].astype(BF16)
            o_ref[rows, cs] = ycg_ref[rows, cs] + sga_ref[rows, cs] * a

    units = [(g, kv) for g in range(groups) for kv in range(N_KV_HEADS)]
    qts = {0: q_transposed(0)}
    pending = scores(0, 0, qts[0])
    outs = []
    for n, (g, kv) in enumerate(units):
        s = pending
        if n + 1 < len(units):
            g2, kv2 = units[n + 1]
            if g2 not in qts:
                qts[g2] = q_transposed(g2)
            pending = scores(g2, kv2, qts[g2])
        outs.append(softmax_pv(g, kv, s))
        if kv == N_KV_HEADS - 1:
            merge(g, outs)
            outs = []
            del qts[g]


def _attention(qs, k_prev_arr, k_cur_arr, v_prev_arr, v_cur_arr, sink_w, ycg, sga,
               *, n_seq, tile, tiles_per_seq, prev_map, n_new):
    n = qs.shape[0]
    groups = tile // (ATT_TOK if n_new is None else n_new)
    cur = lambda i, j: (i * tiles_per_seq + j, 0)
    const2 = lambda i, j: (0, 0)
    kern = functools.partial(_attention_kernel, groups=groups, n_new=n_new)
    return pl.pallas_call(
        kern,
        grid=(n_seq, tiles_per_seq),
        in_specs=[
            pl.BlockSpec((tile, D_MODEL), cur),
            pl.BlockSpec((WINDOW, KV_DIM), prev_map),
            pl.BlockSpec((tile, KV_DIM), cur),
            pl.BlockSpec((WINDOW, KV_DIM), prev_map),
            pl.BlockSpec((tile, KV_DIM), cur),
            pl.BlockSpec(sink_w.shape, const2),
            pl.BlockSpec((tile, D_MODEL), cur),
            pl.BlockSpec((tile, D_MODEL), cur),
        ],
        out_specs=pl.BlockSpec((tile, D_MODEL), cur),
        out_shape=jax.ShapeDtypeStruct((n, D_MODEL), BF16),
        compiler_params=pltpu.CompilerParams(
            dimension_semantics=("arbitrary", "arbitrary"),
            vmem_limit_bytes=VMEM_LIMIT_BYTES),
        name="attention",
    )(qs, k_prev_arr, k_cur_arr, v_prev_arr, v_cur_arr, sink_w, ycg, sga)


def _out_ffn_kernel(m_ref, x_ref, wo_ref, nw_ref, wg_ref, wu_ref, wd_ref, y_ref):
    x1 = x_ref[...] + jnp.dot(m_ref[...], wo_ref[...], preferred_element_type=F32)
    xn = (x1 * _rms_scale(x1) * nw_ref[...]).astype(BF16)
    acc = x1
    for lo, hi in FF_BLOCKS:
        g = jnp.dot(xn, wg_ref[:, lo:hi], preferred_element_type=F32)
        u = jnp.dot(xn, wu_ref[:, lo:hi], preferred_element_type=F32)
        a = (jax.nn.silu(g) * u).astype(BF16)
        acc = acc + jnp.dot(a, wd_ref[lo:hi, :], preferred_element_type=F32)
    y_ref[...] = acc


def _out_ffn(merged, x2d, wo, nw, wg, wu, wd, *, tile):
    n = x2d.shape[0]
    row = lambda i: (i, 0)
    const2 = lambda i: (0, 0)
    resident = dict(pipeline_mode=pl.Buffered(1))
    return pl.pallas_call(
        _out_ffn_kernel,
        grid=(n // tile,),
        in_specs=[
            pl.BlockSpec((tile, D_MODEL), row),
            pl.BlockSpec((tile, D_MODEL), row),
            pl.BlockSpec(wo.shape, const2, **resident),
            pl.BlockSpec((1, D_MODEL), const2),
            pl.BlockSpec(wg.shape, const2, **resident),
            pl.BlockSpec(wu.shape, const2, **resident),
            pl.BlockSpec(wd.shape, const2, **resident),
        ],
        out_specs=pl.BlockSpec((tile, D_MODEL), row),
        out_shape=jax.ShapeDtypeStruct((n, D_MODEL), F32),
        compiler_params=pltpu.CompilerParams(
            dimension_semantics=("arbitrary",),
            vmem_limit_bytes=VMEM_LIMIT_BYTES),
        name="out_ffn",
    )(merged, x2d, wo, nw, wg, wu, wd)


def _layer(xp, xs, state_conv, cache_k, cache_v, norm_mix_w, w_in, conv_w, q_norm_w, k_norm_w,
           sinks, w_out, norm_ffn_w, w_gate, w_up, w_down):
    b_p, t_p, _ = xp.shape
    b_s, t_s, _ = xs.shape

    hs, bs, cs_, qs_, ks_, vs_, gcs, gas = (0, D_CONV, 2 * D_CONV, 3 * D_CONV, 3 * D_CONV + D_MODEL,
                                            3 * D_CONV + D_MODEL + KV_DIM,
                                            3 * D_CONV + D_MODEL + 2 * KV_DIM,
                                            4 * D_CONV + D_MODEL + 2 * KV_DIM)
    w_in_b = w_in.astype(BF16)
    w1 = jnp.stack([
        jnp.concatenate([w_in_b[:, s + cb * COL_BLOCK:s + (cb + 1) * COL_BLOCK] for s in (hs, bs, cs_, gcs)],
                        axis=1)
        for cb in range(N_COL_BLOCKS)])
    w2 = jnp.concatenate([w_in_b[:, qs_:qs_ + D_MODEL], w_in_b[:, ks_:ks_ + KV_DIM],
                          w_in_b[:, vs_:vs_ + KV_DIM], w_in_b[:, gas:gas + D_MODEL]], axis=1)
    head_id = jnp.arange(MXU_DIM) // HEAD_DIM
    bd = ((head_id[:, None] == head_id[None, :]) * (1.0 / HEAD_DIM)).astype(BF16)
    nw = norm_mix_w.reshape(1, D_MODEL)
    qw = (jnp.tile(q_norm_w, N_HEADS) * (SCALE * LOG2E)).reshape(1, D_MODEL)
    kw = jnp.tile(k_norm_w, N_KV_HEADS).reshape(1, KV_DIM)
    sink_w = jnp.broadcast_to((sinks.astype(F32) * LOG2E).reshape(N_KV_HEADS, GROUP, 1),
                              (N_KV_HEADS, GROUP, ATT_TOK)).reshape(N_KV_HEADS, GROUP * ATT_TOK)
    wo = w_out.astype(BF16)
    nfw = norm_ffn_w.reshape(1, D_MODEL)
    wg, wu, wd = w_gate.astype(BF16), w_up.astype(BF16), w_down.astype(BF16)

    xp2 = xp.reshape(b_p * t_p, D_MODEL)
    zero_prefix = jnp.zeros((b_p, CONV_W - 1, D_CONV), F32)
    ycg, q, sga, kh, v, cnew_p = _mixer_in(xp2, zero_prefix, nw, w1, w2, conv_w, bd, qw, kw,
                                           tile=512, seq_len=t_p)
    att_tile = 512
    blocks_per_tile = att_tile // WINDOW
    blocks_per_seq = t_p // WINDOW
    merged = _attention(
        q, kh, kh, v, v, sink_w, ycg, sga, n_seq=b_p, tile=att_tile, tiles_per_seq=t_p // att_tile,
        prev_map=lambda i, j: (i * blocks_per_seq + jnp.maximum(j * blocks_per_tile - 1, 0), 0),
        n_new=None)
    yp = _out_ffn(merged, xp2, wo, nfw, wg, wu, wd, tile=512).reshape(b_p, t_p, D_MODEL)
    tail = lambda a: a.reshape(b_p, t_p, KV_DIM)[:, -WINDOW:].reshape(b_p, WINDOW, N_KV_HEADS, HEAD_DIM)
    k_p, v_p = tail(kh), tail(v)

    xs2 = xs.reshape(b_s * t_s, D_MODEL)
    ycg_s, q_s, sga_s, kh_s, v_s, cnew_s = _mixer_in(xs2, state_conv, nw, w1, w2, conv_w, bd, qw, kw,
                                                     tile=b_s * t_s, seq_len=t_s)
    cache_len = cache_k.shape[1]
    merged_s = _attention(
        q_s, cache_k.reshape(b_s * cache_len, KV_DIM), kh_s,
        cache_v.reshape(b_s * cache_len, KV_DIM), v_s, sink_w, ycg_s, sga_s,
        n_seq=b_s, tile=t_s, tiles_per_seq=1, prev_map=lambda i, j: (i, 0), n_new=t_s)
    ys = _out_ffn(merged_s, xs2, wo, nfw, wg, wu, wd, tile=b_s * t_s).reshape(b_s, t_s, D_MODEL)
    k_s = kh_s.reshape(b_s, t_s, N_KV_HEADS, HEAD_DIM)
    v_s = v_s.reshape(b_s, t_s, N_KV_HEADS, HEAD_DIM)
    return yp, ys, cnew_p, k_p, v_p, cnew_s, k_s, v_s


def kernel(x_prompt, x_sample, state_conv, cache_k, cache_v, norm_mix_w, w_in, conv_w, q_norm_w, k_norm_w, sinks, w_out, norm_ffn_w, w_gate, w_up, w_down):
    depth = w_in.shape[0]
    assert cache_k.shape[2] == WINDOW, "sample attention assumes a full window of cached rows"
    yp, ys = x_prompt, x_sample
    outs = [[] for _ in range(6)]
    for l in range(depth):
        yp, ys, *rest = _layer(
            yp, ys, state_conv[l], cache_k[l], cache_v[l], norm_mix_w[l], w_in[l], conv_w[l],
            q_norm_w[l], k_norm_w[l], sinks[l], w_out[l], norm_ffn_w[l], w_gate[l], w_up[l], w_down[l])
        for acc, val in zip(outs, rest):
            acc.append(val)
    return (yp, ys) + tuple(jnp.stack(o) for o in outs)
```

```python
import functools

import jax
import jax.numpy as jnp
from jax import lax
from jax.experimental import pallas as pl
from jax.experimental.pallas import tpu as pltpu

D_MODEL = 1024
HEAD_DIM = 64
N_HEADS = 16
N_KV_HEADS = 4
GROUP = N_HEADS // N_KV_HEADS
KV_DIM = N_KV_HEADS * HEAD_DIM
D_CONV = D_MODEL
CONV_W = 3
D_FF = 2816
CHUNK = 64
WINDOW = 128
EPS = 1e-6
NEG_INF = -1e30
SCALE = HEAD_DIM ** -0.5

LANES = 128
SUBLANES = 8
MXU_DIM = 256
VMEM_LIMIT_BYTES = 56 * 1024 * 1024

COL_BLOCK = MXU_DIM
N_COL_BLOCKS = D_CONV // COL_BLOCK
ATT_TOK = 2 * CHUNK
ATT_KEYS = ATT_TOK + WINDOW
FF_BLOCKS = ((0, 1024), (1024, 2048), (2048, D_FF))
_QKV = D_MODEL + 2 * KV_DIM
W2_SLABS = ((0, _QKV), (_QKV, _QKV + 3 * MXU_DIM), (_QKV + 3 * MXU_DIM, _QKV + D_MODEL))
LOG2E = 1.4426950408889634
SCORE_LOOKAHEAD = 2

BF16 = jnp.bfloat16
F32 = jnp.float32


def _rms_scale(x):
    return lax.rsqrt(jnp.mean(x * x, axis=-1, keepdims=True) + EPS)


def _mixer_in_kernel(x_ref, pre_ref, nw_ref, w1_ref, w2_ref, cw_ref, bd_ref, qw_ref, kw_ref,
                     ycg_ref, qs_ref, sga_ref, kh_ref, v_ref, cnew_ref, s_ref,
                     *, seqs_per_tile, rows_per_seq, tiles_per_seq):
    j = pl.program_id(1)
    slot = rows_per_seq + SUBLANES

    def load_prefix():
        for b in range(seqs_per_tile):
            s_ref[b * slot + 6:b * slot + 8, :] = pre_ref[b]

    if tiles_per_seq == 1:
        load_prefix()
    else:
        pl.when(j == 0)(load_prefix)

    x = x_ref[...]
    xn = (x * _rms_scale(x) * nw_ref[...]).astype(BF16)

    def project(slab):
        if slab < N_COL_BLOCKS:
            return jnp.dot(xn, w1_ref[slab], preferred_element_type=F32)
        lo, hi = W2_SLABS[slab - N_COL_BLOCKS]
        return jnp.dot(xn, w2_ref[:, lo:hi], preferred_element_type=F32)

    qkv = project(N_COL_BLOCKS)
    pending = project(0)
    bd = bd_ref[...]

    def head_scale(t):
        ms = jnp.dot((t * t).astype(BF16), bd, preferred_element_type=F32)
        return lax.rsqrt(ms + EPS)

    for c in range(D_MODEL // MXU_DIM):
        cs = slice(c * MXU_DIM, (c + 1) * MXU_DIM)
        qb = qkv[:, cs]
        qs_ref[:, cs] = (qb * head_scale(qb) * qw_ref[:, cs]).astype(BF16)
    k = qkv[:, D_MODEL:D_MODEL + KV_DIM]
    kh_ref[...] = k * head_scale(k) * kw_ref[...]
    v_ref[...] = qkv[:, D_MODEL + KV_DIM:D_MODEL + 2 * KV_DIM]

    for cb in range(N_COL_BLOCKS):
        cs = slice(cb * COL_BLOCK, (cb + 1) * COL_BLOCK)
        r = pending
        pending = project(cb + 1 if cb + 1 < N_COL_BLOCKS else N_COL_BLOCKS + 1)
        h = r[:, 0 * COL_BLOCK:1 * COL_BLOCK]
        gb = r[:, 1 * COL_BLOCK:2 * COL_BLOCK]
        gc = r[:, 2 * COL_BLOCK:3 * COL_BLOCK]
        gcv = r[:, 3 * COL_BLOCK:4 * COL_BLOCK]
        u = gc * h
        for b in range(seqs_per_tile):
            rs = slice(b * rows_per_seq, (b + 1) * rows_per_seq)
            base = b * slot + SUBLANES
            ub = u[rs]
            s_ref[base:base + rows_per_seq, cs] = ub
            um1 = s_ref[base - 1:base - 1 + rows_per_seq, cs]
            um2 = s_ref[base - 2:base - 2 + rows_per_seq, cs]
            y = cw_ref[0:1, cs] * um2 + cw_ref[1:2, cs] * um1 + cw_ref[2:3, cs] * ub
            ycg_ref[rs, cs] = (jax.nn.sigmoid(gcv[rs]) * (gb[rs] * y)).astype(BF16)

    for b in range(seqs_per_tile):
        end = b * slot + SUBLANES + rows_per_seq
        cnew_ref[b] = s_ref[end - 2:end, :]
    if tiles_per_seq > 1:
        end = SUBLANES + rows_per_seq
        s_ref[6:8, :] = s_ref[end - 2:end, :]

    ga_head = pending
    pending = project(N_COL_BLOCKS + 2)
    split = W2_SLABS[1][1] - W2_SLABS[1][0]
    sga_ref[:, :split] = jax.nn.sigmoid(ga_head).astype(BF16)
    sga_ref[:, split:] = jax.nn.sigmoid(pending).astype(BF16)


def _mixer_in(x2d, prefix, nw, w1, w2, cw, bd, qw, kw, *, tile, seq_len):
    n = x2d.shape[0]
    n_seq = n // seq_len
    if tile >= seq_len:
        seqs_per_tile, rows_per_seq, tiles_per_seq = tile // seq_len, seq_len, 1
    else:
        seqs_per_tile, rows_per_seq, tiles_per_seq = 1, tile, seq_len // tile
    assert seqs_per_tile * rows_per_seq == tile and n % tile == 0
    grid = (n_seq // seqs_per_tile, tiles_per_seq)
    row = lambda i, j: (i * tiles_per_seq + j, 0)
    const2 = lambda i, j: (0, 0)
    const3 = lambda i, j: (0, 0, 0)
    seq3 = lambda i, j: (i, 0, 0)
    resident = dict(pipeline_mode=pl.Buffered(1))
    kern = functools.partial(_mixer_in_kernel, seqs_per_tile=seqs_per_tile,
                             rows_per_seq=rows_per_seq, tiles_per_seq=tiles_per_seq)
    return pl.pallas_call(
        kern,
        grid=grid,
        in_specs=[
            pl.BlockSpec((tile, D_MODEL), row),
            pl.BlockSpec((seqs_per_tile, CONV_W - 1, D_CONV), seq3),
            pl.BlockSpec((1, D_MODEL), const2),
            pl.BlockSpec(w1.shape, const3, **resident),
            pl.BlockSpec(w2.shape, const2, **resident),
            pl.BlockSpec(cw.shape, const2),
            pl.BlockSpec(bd.shape, const2),
            pl.BlockSpec(qw.shape, const2),
            pl.BlockSpec(kw.shape, const2),
        ],
        out_specs=[
            pl.BlockSpec((tile, D_CONV), row),
            pl.BlockSpec((tile, D_MODEL), row),
            pl.BlockSpec((tile, D_MODEL), row),
            pl.BlockSpec((tile, KV_DIM), row),
            pl.BlockSpec((tile, KV_DIM), row),
            pl.BlockSpec((seqs_per_tile, CONV_W - 1, D_CONV), seq3),
        ],
        out_shape=[
            jax.ShapeDtypeStruct((n, D_CONV), BF16),
            jax.ShapeDtypeStruct((n, D_MODEL), BF16),
            jax.ShapeDtypeStruct((n, D_MODEL), BF16),
            jax.ShapeDtypeStruct((n, KV_DIM), F32),
            jax.ShapeDtypeStruct((n, KV_DIM), F32),
            jax.ShapeDtypeStruct((n_seq, CONV_W - 1, D_CONV), F32),
        ],
        scratch_shapes=[pltpu.VMEM((seqs_per_tile * (rows_per_seq + SUBLANES), D_CONV), F32)],
        compiler_params=pltpu.CompilerParams(
            dimension_semantics=("arbitrary", "arbitrary"),
            vmem_limit_bytes=VMEM_LIMIT_BYTES),
        name="mixer_in",
    )(x2d, prefix, nw, w1, w2, cw, bd, qw, kw)


def _attention_kernel(q_ref, kp_ref, kc_ref, vp_ref, vc_ref, sink_ref, ycg_ref, sga_ref, o_ref,
                      *, groups, n_new):
    j = pl.program_id(1)
    wide = GROUP * ATT_TOK

    own = ATT_TOK if n_new is None else n_new

    def own_rows(ref, g, cols=slice(None)):
        blk = ref[g * own:(g + 1) * own, cols]
        if own == ATT_TOK:
            return blk
        return jnp.concatenate([blk, jnp.zeros((ATT_TOK - own, blk.shape[1]), blk.dtype)], axis=0)

    def key_block(ref_prev, ref_cur, i):
        return ref_prev[...] if i == 0 else own_rows(ref_cur, i - 1)

    kb = [key_block(kp_ref, kc_ref, i).astype(BF16) for i in range(groups + 1)]
    vt = [key_block(vp_ref, vc_ref, i).T.astype(BF16) for i in range(groups + 1)]
    ones = jnp.ones((2 * SUBLANES, ATT_KEYS), BF16)
    zeros = jnp.zeros((HEAD_DIM, wide), BF16)
    first_half = lax.broadcasted_iota(jnp.int32, (CHUNK, ATT_TOK), 1) < CHUNK
    has_prev = j > 0

    def scores(g, kv, qt):
        qcat = jnp.concatenate(
            [qt[2 * kv + i // 2][(i % 2) * HEAD_DIM:(i % 2 + 1) * HEAD_DIM] for i in range(GROUP)],
            axis=1)
        rhs = jnp.concatenate([qcat, zeros] if kv % 2 == 0 else [zeros, qcat], axis=0)
        slab = slice((kv // 2) * LANES, (kv // 2 + 1) * LANES)
        kcat = jnp.concatenate([kb[g][:, slab], kb[g + 1][:, slab]], axis=0)
        return jnp.dot(kcat, rhs, preferred_element_type=F32)

    def masked(g, s):
        blk = [s[i * CHUNK:(i + 1) * CHUNK] for i in range(ATT_KEYS // CHUNK)]
        if n_new is None:
            if g == 0:
                blk[0] = jnp.where(first_half & has_prev, blk[0], NEG_INF)
                blk[1] = jnp.where(has_prev, blk[1], NEG_INF)
            else:
                blk[0] = jnp.where(first_half, blk[0], NEG_INF)
            blk[3] = jnp.where(first_half, NEG_INF, blk[3])
        else:
            row = lax.broadcasted_iota(jnp.int32, (CHUNK, ATT_TOK), 0)
            blk[2] = jnp.where(row < n_new, blk[2], NEG_INF)
            blk[3] = jnp.full_like(blk[3], NEG_INF)
        return jnp.concatenate(blk, axis=0)

    def softmax_pv(g, kv, s):
        es, sink_terms = [], []
        for i in range(GROUP):
            lanes = slice(i * ATT_TOK, (i + 1) * ATT_TOK)
            si = masked(g, s[:, lanes])
            sink = sink_ref[kv:kv + 1, lanes]
            m = jnp.maximum(jnp.max(si, axis=0, keepdims=True), sink)
            es.append(jnp.exp2(si - m).astype(BF16))
            sink_terms.append(jnp.exp2(sink - m))
        e = jnp.concatenate(es, axis=1)
        vrows = slice(kv * HEAD_DIM, (kv + 1) * HEAD_DIM)
        lhs = jnp.concatenate([vt[g][vrows], vt[g + 1][vrows]], axis=1)
        oe = jnp.dot(jnp.concatenate([lhs, ones], axis=0), e, preferred_element_type=F32)
        den = oe[HEAD_DIM:HEAD_DIM + 1] + jnp.concatenate(sink_terms, axis=1)
        return oe[:HEAD_DIM] * (1.0 / den)

    def q_transposed(g):
        return [own_rows(q_ref, g, slice(p * LANES, (p + 1) * LANES)).astype(F32).T.astype(BF16)
                for p in range(D_MODEL // LANES)]

    def merge(g, outs):
        rows = slice(g * own, (g + 1) * own)
        for p in range(D_MODEL // LANES):
            cs = slice(p * LANES, (p + 1) * LANES)
            pair = []
            for head in (2 * p, 2 * p + 1):
                i = head % GROUP
                pair.append(outs[head // GROUP][:, i * ATT_TOK:(i + 1) * ATT_TOK])
            a = jnp.concatenate(pair, axis=0).T[:own].astype(BF16)
            o_ref[rows, cs] = ycg_ref[rows, cs] + sga_ref[rows, cs] * a

    units = [(g, kv) for g in range(groups) for kv in range(N_KV_HEADS)]
    qts = {}
    pending = []

    def issue(n):
        g2, kv2 = units[n]
        if g2 not in qts:
            qts[g2] = q_transposed(g2)
        pending.append(scores(g2, kv2, qts[g2]))

    for n in range(min(SCORE_LOOKAHEAD, len(units))):
        issue(n)
    outs = []
    for n, (g, kv) in enumerate(units):
        if n + SCORE_LOOKAHEAD < len(units):
            issue(n + SCORE_LOOKAHEAD)
        outs.append(softmax_pv(g, kv, pending.pop(0)))
        if kv == N_KV_HEADS - 1:
            merge(g, outs)
            outs = []


def _attention(qs, k_prev_arr, k_cur_arr, v_prev_arr, v_cur_arr, sink_w, ycg, sga,
               *, n_seq, tile, tiles_per_seq, prev_map, n_new):
    n = qs.shape[0]
    groups = tile // (ATT_TOK if n_new is None else n_new)
    cur = lambda i, j: (i * tiles_per_seq + j, 0)
    const2 = lambda i, j: (0, 0)
    kern = functools.partial(_attention_kernel, groups=groups, n_new=n_new)
    return pl.pallas_call(
        kern,
        grid=(n_seq, tiles_per_seq),
        in_specs=[
            pl.BlockSpec((tile, D_MODEL), cur),
            pl.BlockSpec((WINDOW, KV_DIM), prev_map),
            pl.BlockSpec((tile, KV_DIM), cur),
            pl.BlockSpec((WINDOW, KV_DIM), prev_map),
            pl.BlockSpec((tile, KV_DIM), cur),
            pl.BlockSpec(sink_w.shape, const2),
            pl.BlockSpec((tile, D_MODEL), cur),
            pl.BlockSpec((tile, D_MODEL), cur),
        ],
        out_specs=pl.BlockSpec((tile, D_MODEL), cur),
        out_shape=jax.ShapeDtypeStruct((n, D_MODEL), BF16),
        compiler_params=pltpu.CompilerParams(
            dimension_semantics=("arbitrary", "arbitrary"),
            vmem_limit_bytes=VMEM_LIMIT_BYTES),
        name="attention",
    )(qs, k_prev_arr, k_cur_arr, v_prev_arr, v_cur_arr, sink_w, ycg, sga)


def _out_ffn_kernel(m_ref, x_ref, wo_ref, nw_ref, wg_ref, wu_ref, wd_ref, y_ref):
    x1 = x_ref[...] + jnp.dot(m_ref[...], wo_ref[...], preferred_element_type=F32)
    xn = (x1 * _rms_scale(x1) * nw_ref[...]).astype(BF16)
    acc = x1
    for lo, hi in FF_BLOCKS:
        g = jnp.dot(xn, wg_ref[:, lo:hi], preferred_element_type=F32)
        u = jnp.dot(xn, wu_ref[:, lo:hi], preferred_element_type=F32)
        a = (jax.nn.silu(g) * u).astype(BF16)
        acc = acc + jnp.dot(a, wd_ref[lo:hi, :], preferred_element_type=F32)
    y_ref[...] = acc


def _out_ffn(merged, x2d, wo, nw, wg, wu, wd, *, tile):
    n = x2d.shape[0]
    row = lambda i: (i, 0)
    const2 = lambda i: (0, 0)
    resident = dict(pipeline_mode=pl.Buffered(1))
    return pl.pallas_call(
        _out_ffn_kernel,
        grid=(n // tile,),
        in_specs=[
            pl.BlockSpec((tile, D_MODEL), row),
            pl.BlockSpec((tile, D_MODEL), row),
            pl.BlockSpec(wo.shape, const2, **resident),
            pl.BlockSpec((1, D_MODEL), const2),
            pl.BlockSpec(wg.shape, const2, **resident),
            pl.BlockSpec(wu.shape, const2, **resident),
            pl.BlockSpec(wd.shape, const2, **resident),
        ],
        out_specs=pl.BlockSpec((tile, D_MODEL), row),
        out_shape=jax.ShapeDtypeStruct((n, D_MODEL), F32),
        compiler_params=pltpu.CompilerParams(
            dimension_semantics=("arbitrary",),
            vmem_limit_bytes=VMEM_LIMIT_BYTES),
        name="out_ffn",
    )(merged, x2d, wo, nw, wg, wu, wd)


def _layer(xp, xs, state_conv, cache_k, cache_v, norm_mix_w, w_in, conv_w, q_norm_w, k_norm_w,
           sinks, w_out, norm_ffn_w, w_gate, w_up, w_down):
    b_p, t_p, _ = xp.shape
    b_s, t_s, _ = xs.shape

    hs, bs, cs_, qs_, ks_, vs_, gcs, gas = (0, D_CONV, 2 * D_CONV, 3 * D_CONV, 3 * D_CONV + D_MODEL,
                                            3 * D_CONV + D_MODEL + KV_DIM,
                                            3 * D_CONV + D_MODEL + 2 * KV_DIM,
                                            4 * D_CONV + D_MODEL + 2 * KV_DIM)
    w_in_b = w_in.astype(BF16)
    w1 = jnp.stack([
        jnp.concatenate([w_in_b[:, s + cb * COL_BLOCK:s + (cb + 1) * COL_BLOCK] for s in (hs, bs, cs_, gcs)],
                        axis=1)
        for cb in range(N_COL_BLOCKS)])
    w2 = jnp.concatenate([w_in_b[:, qs_:qs_ + D_MODEL], w_in_b[:, ks_:ks_ + KV_DIM],
                          w_in_b[:, vs_:vs_ + KV_DIM], w_in_b[:, gas:gas + D_MODEL]], axis=1)
    head_id = jnp.arange(MXU_DIM) // HEAD_DIM
    bd = ((head_id[:, None] == head_id[None, :]) * (1.0 / HEAD_DIM)).astype(BF16)
    nw = norm_mix_w.reshape(1, D_MODEL)
    qw = (jnp.tile(q_norm_w, N_HEADS) * (SCALE * LOG2E)).reshape(1, D_MODEL)
    kw = jnp.tile(k_norm_w, N_KV_HEADS).reshape(1, KV_DIM)
    sink_w = jnp.broadcast_to((sinks.astype(F32) * LOG2E).reshape(N_KV_HEADS, GROUP, 1),
                              (N_KV_HEADS, GROUP, ATT_TOK)).reshape(N_KV_HEADS, GROUP * ATT_TOK)
    wo = w_out.astype(BF16)
    nfw = norm_ffn_w.reshape(1, D_MODEL)
    wg, wu, wd = w_gate.astype(BF16), w_up.astype(BF16), w_down.astype(BF16)

    xp2 = xp.reshape(b_p * t_p, D_MODEL)
    zero_prefix = jnp.zeros((b_p, CONV_W - 1, D_CONV), F32)
    ycg, q, sga, kh, v, cnew_p = _mixer_in(xp2, zero_prefix, nw, w1, w2, conv_w, bd, qw, kw,
                                           tile=512, seq_len=t_p)
    att_tile = 1024
    blocks_per_tile = att_tile // WINDOW
    blocks_per_seq = t_p // WINDOW
    merged = _attention(
        q, kh, kh, v, v, sink_w, ycg, sga, n_seq=b_p, tile=att_tile, tiles_per_seq=t_p // att_tile,
        prev_map=lambda i, j: (i * blocks_per_seq + jnp.maximum(j * blocks_per_tile - 1, 0), 0),
        n_new=None)
    yp = _out_ffn(merged, xp2, wo, nfw, wg, wu, wd, tile=512).reshape(b_p, t_p, D_MODEL)
    tail = lambda a: a.reshape(b_p, t_p, KV_DIM)[:, -WINDOW:].reshape(b_p, WINDOW, N_KV_HEADS, HEAD_DIM)
    k_p, v_p = tail(kh), tail(v)

    xs2 = xs.reshape(b_s * t_s, D_MODEL)
    ycg_s, q_s, sga_s, kh_s, v_s, cnew_s = _mixer_in(xs2, state_conv, nw, w1, w2, conv_w, bd, qw, kw,
                                                     tile=b_s * t_s, seq_len=t_s)
    cache_len = cache_k.shape[1]
    merged_s = _attention(
        q_s, cache_k.reshape(b_s * cache_len, KV_DIM), kh_s,
        cache_v.reshape(b_s * cache_len, KV_DIM), v_s, sink_w, ycg_s, sga_s,
        n_seq=b_s, tile=t_s, tiles_per_seq=1, prev_map=lambda i, j: (i, 0), n_new=t_s)
    ys = _out_ffn(merged_s, xs2, wo, nfw, wg, wu, wd, tile=b_s * t_s).reshape(b_s, t_s, D_MODEL)
    k_s = kh_s.reshape(b_s, t_s, N_KV_HEADS, HEAD_DIM)
    v_s = v_s.reshape(b_s, t_s, N_KV_HEADS, HEAD_DIM)
    return yp, ys, cnew_p, k_p, v_p, cnew_s, k_s, v_s


def kernel(x_prompt, x_sample, state_conv, cache_k, cache_v, norm_mix_w, w_in, conv_w, q_norm_w, k_norm_w, sinks, w_out, norm_ffn_w, w_gate, w_up, w_down):
    depth = w_in.shape[0]
    assert cache_k.shape[2] == WINDOW, "sample attention assumes a full window of cached rows"
    yp, ys = x_prompt, x_sample
    outs = [[] for _ in range(6)]
    for l in range(depth):
        yp, ys, *rest = _layer(
            yp, ys, state_conv[l], cache_k[l], cache_v[l], norm_mix_w[l], w_in[l], conv_w[l],
            q_norm_w[l], k_norm_w[l], sinks[l], w_out[l], norm_ffn_w[l], w_gate[l], w_up[l], w_down[l])
        for acc, val in zip(outs, rest):
            acc.append(val)
    return (yp, ys) + tuple(jnp.stack(o) for o in outs)
```

```python
import functools

import jax
import jax.numpy as jnp
from jax import lax
from jax.experimental import pallas as pl
from jax.experimental.pallas import tpu as pltpu

D_MODEL = 1024
HEAD_DIM = 64
N_HEADS = 16
N_KV_HEADS = 4
GROUP = N_HEADS // N_KV_HEADS
KV_DIM = N_KV_HEADS * HEAD_DIM
D_CONV = D_MODEL
CONV_W = 3
D_FF = 2816
CHUNK = 64
WINDOW = 128
EPS = 1e-6
NEG_INF = -1e30
SCALE = HEAD_DIM ** -0.5

LANES = 128
SUBLANES = 8
MXU_DIM = 256
VMEM_LIMIT_BYTES = 56 * 1024 * 1024

COL_BLOCK = MXU_DIM
N_COL_BLOCKS = D_CONV // COL_BLOCK
ATT_TOK = 2 * CHUNK
ATT_KEYS = ATT_TOK + WINDOW
FF_BLOCKS = ((0, D_FF),)
_QKV = D_MODEL + 2 * KV_DIM
W2_SLABS = ((0, _QKV), (_QKV, _QKV + 3 * MXU_DIM), (_QKV + 3 * MXU_DIM, _QKV + D_MODEL))
LOG2E = 1.4426950408889634
SAMPLE_SEQS_PER_STEP = 4
SCORE_LOOKAHEAD = 2

BF16 = jnp.bfloat16
F32 = jnp.float32


def _rms_scale(x):
    return lax.rsqrt(jnp.mean(x * x, axis=-1, keepdims=True) + EPS)


def _mixer_in_kernel(x_ref, pre_ref, nw_ref, w1_ref, w2_ref, cw_ref, bd_ref, qw_ref, kw_ref,
                     ycg_ref, qs_ref, sga_ref, kh_ref, v_ref, cnew_ref, s_ref,
                     *, seqs_per_tile, rows_per_seq, tiles_per_seq):
    j = pl.program_id(1)
    slot = rows_per_seq + SUBLANES

    def load_prefix():
        for b in range(seqs_per_tile):
            s_ref[b * slot + 6:b * slot + 8, :] = pre_ref[b]

    if tiles_per_seq == 1:
        load_prefix()
    else:
        pl.when(j == 0)(load_prefix)

    x = x_ref[...]
    xn = (x * _rms_scale(x) * nw_ref[...]).astype(BF16)

    def project(slab):
        if slab < N_COL_BLOCKS:
            return jnp.dot(xn, w1_ref[slab], preferred_element_type=F32)
        lo, hi = W2_SLABS[slab - N_COL_BLOCKS]
        return jnp.dot(xn, w2_ref[:, lo:hi], preferred_element_type=F32)

    qkv = project(N_COL_BLOCKS)
    pending = project(0)
    bd = bd_ref[...]

    def head_scale(t):
        ms = jnp.dot((t * t).astype(BF16), bd, preferred_element_type=F32)
        return lax.rsqrt(ms + EPS)

    for c in range(D_MODEL // MXU_DIM):
        cs = slice(c * MXU_DIM, (c + 1) * MXU_DIM)
        qb = qkv[:, cs]
        qs_ref[:, cs] = (qb * head_scale(qb) * qw_ref[:, cs]).astype(BF16)
    k = qkv[:, D_MODEL:D_MODEL + KV_DIM]
    kh_ref[...] = k * head_scale(k) * kw_ref[...]
    v_ref[...] = qkv[:, D_MODEL + KV_DIM:D_MODEL + 2 * KV_DIM]

    for cb in range(N_COL_BLOCKS):
        cs = slice(cb * COL_BLOCK, (cb + 1) * COL_BLOCK)
        r = pending
        pending = project(cb + 1 if cb + 1 < N_COL_BLOCKS else N_COL_BLOCKS + 1)
        h = r[:, 0 * COL_BLOCK:1 * COL_BLOCK]
        gb = r[:, 1 * COL_BLOCK:2 * COL_BLOCK]
        gc = r[:, 2 * COL_BLOCK:3 * COL_BLOCK]
        gcv = r[:, 3 * COL_BLOCK:4 * COL_BLOCK]
        u = gc * h
        for b in range(seqs_per_tile):
            rs = slice(b * rows_per_seq, (b + 1) * rows_per_seq)
            base = b * slot + SUBLANES
            ub = u[rs]
            s_ref[base:base + rows_per_seq, cs] = ub
            um1 = s_ref[base - 1:base - 1 + rows_per_seq, cs]
            um2 = s_ref[base - 2:base - 2 + rows_per_seq, cs]
            y = cw_ref[0:1, cs] * um2 + cw_ref[1:2, cs] * um1 + cw_ref[2:3, cs] * ub
            ycg_ref[rs, cs] = (jax.nn.sigmoid(gcv[rs]) * (gb[rs] * y)).astype(BF16)

    for b in range(seqs_per_tile):
        end = b * slot + SUBLANES + rows_per_seq
        cnew_ref[b] = s_ref[end - 2:end, :]
    if tiles_per_seq > 1:
        end = SUBLANES + rows_per_seq
        s_ref[6:8, :] = s_ref[end - 2:end, :]

    ga_head = pending
    pending = project(N_COL_BLOCKS + 2)
    split = W2_SLABS[1][1] - W2_SLABS[1][0]
    sga_ref[:, :split] = jax.nn.sigmoid(ga_head).astype(BF16)
    sga_ref[:, split:] = jax.nn.sigmoid(pending).astype(BF16)


def _mixer_in(x2d, prefix, nw, w1, w2, cw, bd, qw, kw, *, tile, seq_len):
    n = x2d.shape[0]
    n_seq = n // seq_len
    if tile >= seq_len:
        seqs_per_tile, rows_per_seq, tiles_per_seq = tile // seq_len, seq_len, 1
    else:
        seqs_per_tile, rows_per_seq, tiles_per_seq = 1, tile, seq_len // tile
    assert seqs_per_tile * rows_per_seq == tile and n % tile == 0
    grid = (n_seq // seqs_per_tile, tiles_per_seq)
    row = lambda i, j: (i * tiles_per_seq + j, 0)
    const2 = lambda i, j: (0, 0)
    const3 = lambda i, j: (0, 0, 0)
    seq3 = lambda i, j: (i, 0, 0)
    resident = dict(pipeline_mode=pl.Buffered(1))
    kern = functools.partial(_mixer_in_kernel, seqs_per_tile=seqs_per_tile,
                             rows_per_seq=rows_per_seq, tiles_per_seq=tiles_per_seq)
    return pl.pallas_call(
        kern,
        grid=grid,
        in_specs=[
            pl.BlockSpec((tile, D_MODEL), row),
            pl.BlockSpec((seqs_per_tile, CONV_W - 1, D_CONV), seq3),
            pl.BlockSpec((1, D_MODEL), const2),
            pl.BlockSpec(w1.shape, const3, **resident),
            pl.BlockSpec(w2.shape, const2, **resident),
            pl.BlockSpec(cw.shape, const2),
            pl.BlockSpec(bd.shape, const2),
            pl.BlockSpec(qw.shape, const2),
            pl.BlockSpec(kw.shape, const2),
        ],
        out_specs=[
            pl.BlockSpec((tile, D_CONV), row),
            pl.BlockSpec((tile, D_MODEL), row),
            pl.BlockSpec((tile, D_MODEL), row),
            pl.BlockSpec((tile, KV_DIM), row),
            pl.BlockSpec((tile, KV_DIM), row),
            pl.BlockSpec((seqs_per_tile, CONV_W - 1, D_CONV), seq3),
        ],
        out_shape=[
            jax.ShapeDtypeStruct((n, D_CONV), BF16),
            jax.ShapeDtypeStruct((n, D_MODEL), BF16),
            jax.ShapeDtypeStruct((n, D_MODEL), BF16),
            jax.ShapeDtypeStruct((n, KV_DIM), F32),
            jax.ShapeDtypeStruct((n, KV_DIM), F32),
            jax.ShapeDtypeStruct((n_seq, CONV_W - 1, D_CONV), F32),
        ],
        scratch_shapes=[pltpu.VMEM((seqs_per_tile * (rows_per_seq + SUBLANES), D_CONV), F32)],
        compiler_params=pltpu.CompilerParams(
            dimension_semantics=("arbitrary", "arbitrary"),
            vmem_limit_bytes=VMEM_LIMIT_BYTES),
        name="mixer_in",
    )(x2d, prefix, nw, w1, w2, cw, bd, qw, kw)


def _attention_kernel(q_ref, kp_ref, kc_ref, vp_ref, vc_ref, sink_ref, ycg_ref, sga_ref, o_ref,
                      *, groups, n_new):
    j = pl.program_id(1)
    wide = GROUP * ATT_TOK

    own = ATT_TOK if n_new is None else n_new

    def own_rows(ref, g, cols=slice(None)):
        blk = ref[g * own:(g + 1) * own, cols]
        if own == ATT_TOK:
            return blk
        return jnp.concatenate([blk, jnp.zeros((ATT_TOK - own, blk.shape[1]), blk.dtype)], axis=0)

    def key_blocks(ref_prev, ref_cur, prep):
        own_blk = [prep(own_rows(ref_cur, g)) for g in range(groups)]
        if n_new is None:
            prev_blk = [prep(ref_prev[...])] + own_blk[:-1]
        else:
            prev_blk = [prep(ref_prev[g * WINDOW:(g + 1) * WINDOW, :]) for g in range(groups)]
        return prev_blk, own_blk

    kb_prev, kb_own = key_blocks(kp_ref, kc_ref, lambda t: t.astype(BF16))
    vt_prev, vt_own = key_blocks(vp_ref, vc_ref, lambda t: t.T.astype(BF16))
    ones = jnp.ones((2 * SUBLANES, ATT_KEYS), BF16)
    zeros = jnp.zeros((HEAD_DIM, wide), BF16)
    first_half = lax.broadcasted_iota(jnp.int32, (CHUNK, ATT_TOK), 1) < CHUNK
    has_prev = j > 0

    def scores(g, kv, qt):
        qcat = jnp.concatenate(
            [qt[2 * kv + i // 2][(i % 2) * HEAD_DIM:(i % 2 + 1) * HEAD_DIM] for i in range(GROUP)],
            axis=1)
        rhs = jnp.concatenate([qcat, zeros] if kv % 2 == 0 else [zeros, qcat], axis=0)
        slab = slice((kv // 2) * LANES, (kv // 2 + 1) * LANES)
        kcat = jnp.concatenate([kb_prev[g][:, slab], kb_own[g][:, slab]], axis=0)
        return jnp.dot(kcat, rhs, preferred_element_type=F32)

    def masked(g, s):
        blk = [s[i * CHUNK:(i + 1) * CHUNK] for i in range(ATT_KEYS // CHUNK)]
        if n_new is None:
            if g == 0:
                blk[0] = jnp.where(first_half & has_prev, blk[0], NEG_INF)
                blk[1] = jnp.where(has_prev, blk[1], NEG_INF)
            else:
                blk[0] = jnp.where(first_half, blk[0], NEG_INF)
            blk[3] = jnp.where(first_half, NEG_INF, blk[3])
        else:
            row = lax.broadcasted_iota(jnp.int32, (CHUNK, ATT_TOK), 0)
            blk[2] = jnp.where(row < n_new, blk[2], NEG_INF)
            blk[3] = jnp.full_like(blk[3], NEG_INF)
        return jnp.concatenate(blk, axis=0)

    def softmax_pv(g, kv, s):
        es, sink_terms = [], []
        for i in range(GROUP):
            lanes = slice(i * ATT_TOK, (i + 1) * ATT_TOK)
            si = masked(g, s[:, lanes])
            sink = sink_ref[kv:kv + 1, lanes]
            m = jnp.maximum(jnp.max(si, axis=0, keepdims=True), sink)
            es.append(jnp.exp2(si - m).astype(BF16))
            sink_terms.append(jnp.exp2(sink - m))
        e = jnp.concatenate(es, axis=1)
        vrows = slice(kv * HEAD_DIM, (kv + 1) * HEAD_DIM)
        lhs = jnp.concatenate([vt_prev[g][vrows], vt_own[g][vrows]], axis=1)
        oe = jnp.dot(jnp.concatenate([lhs, ones], axis=0), e, preferred_element_type=F32)
        den = oe[HEAD_DIM:HEAD_DIM + 1] + jnp.concatenate(sink_terms, axis=1)
        return oe[:HEAD_DIM] * (1.0 / den)

    def q_transposed(g):
        return [own_rows(q_ref, g, slice(p * LANES, (p + 1) * LANES)).astype(F32).T.astype(BF16)
                for p in range(D_MODEL // LANES)]

    def merge(g, outs):
        rows = slice(g * own, (g + 1) * own)
        for p in range(D_MODEL // LANES):
            cs = slice(p * LANES, (p + 1) * LANES)
            pair = []
            for head in (2 * p, 2 * p + 1):
                i = head % GROUP
                pair.append(outs[head // GROUP][:, i * ATT_TOK:(i + 1) * ATT_TOK])
            a = jnp.concatenate(pair, axis=0).T[:own].astype(BF16)
            o_ref[rows, cs] = ycg_ref[rows, cs] + sga_ref[rows, cs] * a

    units = [(g, kv) for g in range(groups) for kv in range(N_KV_HEADS)]
    qts = {}
    pending = []

    def issue(n):
        g2, kv2 = units[n]
        if g2 not in qts:
            qts[g2] = q_transposed(g2)
        pending.append(scores(g2, kv2, qts[g2]))

    for n in range(min(SCORE_LOOKAHEAD, len(units))):
        issue(n)
    outs = []
    for n, (g, kv) in enumerate(units):
        if n + SCORE_LOOKAHEAD < len(units):
            issue(n + SCORE_LOOKAHEAD)
        outs.append(softmax_pv(g, kv, pending.pop(0)))
        if kv == N_KV_HEADS - 1:
            merge(g, outs)
            outs = []


def _attention(qs, k_prev_arr, k_cur_arr, v_prev_arr, v_cur_arr, sink_w, ycg, sga,
               *, n_seq, tile, tiles_per_seq, prev_map, n_new):
    n = qs.shape[0]
    groups = tile // (ATT_TOK if n_new is None else n_new)
    prev_rows = WINDOW if n_new is None else groups * WINDOW
    cur = lambda i, j: (i * tiles_per_seq + j, 0)
    const2 = lambda i, j: (0, 0)
    kern = functools.partial(_attention_kernel, groups=groups, n_new=n_new)
    return pl.pallas_call(
        kern,
        grid=(n_seq, tiles_per_seq),
        in_specs=[
            pl.BlockSpec((tile, D_MODEL), cur),
            pl.BlockSpec((prev_rows, KV_DIM), prev_map),
            pl.BlockSpec((tile, KV_DIM), cur),
            pl.BlockSpec((prev_rows, KV_DIM), prev_map),
            pl.BlockSpec((tile, KV_DIM), cur),
            pl.BlockSpec(sink_w.shape, const2),
            pl.BlockSpec((tile, D_MODEL), cur),
            pl.BlockSpec((tile, D_MODEL), cur),
        ],
        out_specs=pl.BlockSpec((tile, D_MODEL), cur),
        out_shape=jax.ShapeDtypeStruct((n, D_MODEL), BF16),
        compiler_params=pltpu.CompilerParams(
            dimension_semantics=("arbitrary", "arbitrary"),
            vmem_limit_bytes=VMEM_LIMIT_BYTES),
        name="attention",
    )(qs, k_prev_arr, k_cur_arr, v_prev_arr, v_cur_arr, sink_w, ycg, sga)


def _out_ffn_kernel(m_ref, x_ref, wo_ref, nw_ref, wg_ref, wu_ref, wd_ref, y_ref):
    x1 = x_ref[...] + jnp.dot(m_ref[...], wo_ref[...], preferred_element_type=F32)
    xn = (x1 * _rms_scale(x1) * nw_ref[...]).astype(BF16)
    acc = x1
    for lo, hi in FF_BLOCKS:
        g = jnp.dot(xn, wg_ref[:, lo:hi], preferred_element_type=F32)
        u = jnp.dot(xn, wu_ref[:, lo:hi], preferred_element_type=F32)
        a = (jax.nn.silu(g) * u).astype(BF16)
        acc = acc + jnp.dot(a, wd_ref[lo:hi, :], preferred_element_type=F32)
    y_ref[...] = acc


def _out_ffn(merged, x2d, wo, nw, wg, wu, wd, *, tile):
    n = x2d.shape[0]
    row = lambda i: (i, 0)
    const2 = lambda i: (0, 0)
    resident = dict(pipeline_mode=pl.Buffered(1))
    return pl.pallas_call(
        _out_ffn_kernel,
        grid=(n // tile,),
        in_specs=[
            pl.BlockSpec((tile, D_MODEL), row),
            pl.BlockSpec((tile, D_MODEL), row),
            pl.BlockSpec(wo.shape, const2, **resident),
            pl.BlockSpec((1, D_MODEL), const2),
            pl.BlockSpec(wg.shape, const2, **resident),
            pl.BlockSpec(wu.shape, const2, **resident),
            pl.BlockSpec(wd.shape, const2, **resident),
        ],
        out_specs=pl.BlockSpec((tile, D_MODEL), row),
        out_shape=jax.ShapeDtypeStruct((n, D_MODEL), F32),
        compiler_params=pltpu.CompilerParams(
            dimension_semantics=("arbitrary",),
            vmem_limit_bytes=VMEM_LIMIT_BYTES),
        name="out_ffn",
    )(merged, x2d, wo, nw, wg, wu, wd)


def _layer(xp, xs, state_conv, cache_k, cache_v, norm_mix_w, w_in, conv_w, q_norm_w, k_norm_w,
           sinks, w_out, norm_ffn_w, w_gate, w_up, w_down):
    b_p, t_p, _ = xp.shape
    b_s, t_s, _ = xs.shape

    hs, bs, cs_, qs_, ks_, vs_, gcs, gas = (0, D_CONV, 2 * D_CONV, 3 * D_CONV, 3 * D_CONV + D_MODEL,
                                            3 * D_CONV + D_MODEL + KV_DIM,
                                            3 * D_CONV + D_MODEL + 2 * KV_DIM,
                                            4 * D_CONV + D_MODEL + 2 * KV_DIM)
    w_in_b = w_in.astype(BF16)
    w1 = jnp.stack([
        jnp.concatenate([w_in_b[:, s + cb * COL_BLOCK:s + (cb + 1) * COL_BLOCK] for s in (hs, bs, cs_, gcs)],
                        axis=1)
        for cb in range(N_COL_BLOCKS)])
    w2 = jnp.concatenate([w_in_b[:, qs_:qs_ + D_MODEL], w_in_b[:, ks_:ks_ + KV_DIM],
                          w_in_b[:, vs_:vs_ + KV_DIM], w_in_b[:, gas:gas + D_MODEL]], axis=1)
    head_id = jnp.arange(MXU_DIM) // HEAD_DIM
    bd = ((head_id[:, None] == head_id[None, :]) * (1.0 / HEAD_DIM)).astype(BF16)
    nw = norm_mix_w.reshape(1, D_MODEL)
    qw = (jnp.tile(q_norm_w, N_HEADS) * (SCALE * LOG2E)).reshape(1, D_MODEL)
    kw = jnp.tile(k_norm_w, N_KV_HEADS).reshape(1, KV_DIM)
    sink_w = jnp.broadcast_to((sinks.astype(F32) * LOG2E).reshape(N_KV_HEADS, GROUP, 1),
                              (N_KV_HEADS, GROUP, ATT_TOK)).reshape(N_KV_HEADS, GROUP * ATT_TOK)
    wo = w_out.astype(BF16)
    nfw = norm_ffn_w.reshape(1, D_MODEL)
    wg, wu, wd = w_gate.astype(BF16), w_up.astype(BF16), w_down.astype(BF16)

    xp2 = xp.reshape(b_p * t_p, D_MODEL)
    zero_prefix = jnp.zeros((b_p, CONV_W - 1, D_CONV), F32)
    ycg, q, sga, kh, v, cnew_p = _mixer_in(xp2, zero_prefix, nw, w1, w2, conv_w, bd, qw, kw,
                                           tile=512, seq_len=t_p)
    att_tile = 2048
    blocks_per_tile = att_tile // WINDOW
    blocks_per_seq = t_p // WINDOW
    merged = _attention(
        q, kh, kh, v, v, sink_w, ycg, sga, n_seq=b_p, tile=att_tile, tiles_per_seq=t_p // att_tile,
        prev_map=lambda i, j: (i * blocks_per_seq + jnp.maximum(j * blocks_per_tile - 1, 0), 0),
        n_new=None)
    yp = _out_ffn(merged, xp2, wo, nfw, wg, wu, wd, tile=512).reshape(b_p, t_p, D_MODEL)
    tail = lambda a: a.reshape(b_p, t_p, KV_DIM)[:, -WINDOW:].reshape(b_p, WINDOW, N_KV_HEADS, HEAD_DIM)
    k_p, v_p = tail(kh), tail(v)

    xs2 = xs.reshape(b_s * t_s, D_MODEL)
    ycg_s, q_s, sga_s, kh_s, v_s, cnew_s = _mixer_in(xs2, state_conv, nw, w1, w2, conv_w, bd, qw, kw,
                                                     tile=b_s * t_s, seq_len=t_s)
    cache_len = cache_k.shape[1]
    merged_s = _attention(
        q_s, cache_k.reshape(b_s * cache_len, KV_DIM), kh_s,
        cache_v.reshape(b_s * cache_len, KV_DIM), v_s, sink_w, ycg_s, sga_s,
        n_seq=b_s // SAMPLE_SEQS_PER_STEP, tile=SAMPLE_SEQS_PER_STEP * t_s, tiles_per_seq=1,
        prev_map=lambda i, j: (i, 0), n_new=t_s)
    ys = _out_ffn(merged_s, xs2, wo, nfw, wg, wu, wd, tile=b_s * t_s).reshape(b_s, t_s, D_MODEL)
    k_s = kh_s.reshape(b_s, t_s, N_KV_HEADS, HEAD_DIM)
    v_s = v_s.reshape(b_s, t_s, N_KV_HEADS, HEAD_DIM)
    return yp, ys, cnew_p, k_p, v_p, cnew_s, k_s, v_s


def kernel(x_prompt, x_sample, state_conv, cache_k, cache_v, norm_mix_w, w_in, conv_w, q_norm_w, k_norm_w, sinks, w_out, norm_ffn_w, w_gate, w_up, w_down):
    depth = w_in.shape[0]
    assert cache_k.shape[2] == WINDOW, "sample attention assumes a full window of cached rows"
    yp, ys = x_prompt, x_sample
    outs = [[] for _ in range(6)]
    for l in range(depth):
        yp, ys, *rest = _layer(
            yp, ys, state_conv[l], cache_k[l], cache_v[l], norm_mix_w[l], w_in[l], conv_w[l],
            q_norm_w[l], k_norm_w[l], sinks[l], w_out[l], norm_ffn_w[l], w_gate[l], w_up[l], w_down[l])
        for acc, val in zip(outs, rest):
            acc.append(val)
    return (yp, ys) + tuple(jnp.stack(o) for o in outs)
```

```python
import functools

import jax
import jax.numpy as jnp
from jax import lax
from jax.experimental import pallas as pl
from jax.experimental.pallas import tpu as pltpu

D_MODEL = 1024
HEAD_DIM = 64
N_HEADS = 16
N_KV_HEADS = 4
GROUP = N_HEADS // N_KV_HEADS
KV_DIM = N_KV_HEADS * HEAD_DIM
D_CONV = D_MODEL
CONV_W = 3
D_FF = 2816
CHUNK = 64
WINDOW = 128
EPS = 1e-6
NEG_INF = -1e30
SCALE = HEAD_DIM ** -0.5

LANES = 128
SUBLANES = 8
MXU_DIM = 256
VMEM_LIMIT_BYTES = 56 * 1024 * 1024

COL_BLOCK = MXU_DIM
N_COL_BLOCKS = D_CONV // COL_BLOCK
ATT_TOK = 2 * CHUNK
ATT_KEYS = ATT_TOK + WINDOW
FF_BLOCKS = ((0, D_FF),)
_QKV = D_MODEL + 2 * KV_DIM
W2_SLABS = ((0, _QKV), (_QKV, _QKV + 3 * MXU_DIM), (_QKV + 3 * MXU_DIM, _QKV + D_MODEL))
LOG2E = 1.4426950408889634
SAMPLE_SEQS_PER_STEP = 4
SLAB_LOOKAHEAD = 3
SCORE_LOOKAHEAD = 2

BF16 = jnp.bfloat16
F32 = jnp.float32


def _rms_scale(x):
    return lax.rsqrt(jnp.mean(x * x, axis=-1, keepdims=True) + EPS)


def _mixer_in_kernel(x_ref, pre_ref, nw_ref, w1_ref, w2_ref, cw_ref, bd_ref, qw_ref, kw_ref,
                     ycg_ref, qs_ref, sga_ref, kh_ref, v_ref, cnew_ref, s_ref,
                     *, seqs_per_tile, rows_per_seq, tiles_per_seq):
    j = pl.program_id(1)
    slot = rows_per_seq + SUBLANES

    def load_prefix():
        for b in range(seqs_per_tile):
            s_ref[b * slot + 6:b * slot + 8, :] = pre_ref[b]

    if tiles_per_seq == 1:
        load_prefix()
    else:
        pl.when(j == 0)(load_prefix)

    x = x_ref[...]
    xn = (x * _rms_scale(x) * nw_ref[...]).astype(BF16)

    def project(slab):
        if slab < N_COL_BLOCKS:
            return jnp.dot(xn, w1_ref[slab], preferred_element_type=F32)
        lo, hi = W2_SLABS[slab - N_COL_BLOCKS]
        return jnp.dot(xn, w2_ref[:, lo:hi], preferred_element_type=F32)

    order = [N_COL_BLOCKS] + list(range(N_COL_BLOCKS)) + [N_COL_BLOCKS + 1, N_COL_BLOCKS + 2]
    queue = []

    def take():
        while order and len(queue) <= SLAB_LOOKAHEAD:
            queue.append(project(order.pop(0)))
        return queue.pop(0)

    qkv = take()
    bd = bd_ref[...]

    def head_scale(t):
        ms = jnp.dot((t * t).astype(BF16), bd, preferred_element_type=F32)
        return lax.rsqrt(ms + EPS)

    for c in range(D_MODEL // MXU_DIM):
        cs = slice(c * MXU_DIM, (c + 1) * MXU_DIM)
        qb = qkv[:, cs]
        qs_ref[:, cs] = (qb * head_scale(qb) * qw_ref[:, cs]).astype(BF16)
    k = qkv[:, D_MODEL:D_MODEL + KV_DIM]
    kh_ref[...] = k * head_scale(k) * kw_ref[...]
    v_ref[...] = qkv[:, D_MODEL + KV_DIM:D_MODEL + 2 * KV_DIM]

    for cb in range(N_COL_BLOCKS):
        cs = slice(cb * COL_BLOCK, (cb + 1) * COL_BLOCK)
        r = take()
        h = r[:, 0 * COL_BLOCK:1 * COL_BLOCK]
        gb = r[:, 1 * COL_BLOCK:2 * COL_BLOCK]
        gc = r[:, 2 * COL_BLOCK:3 * COL_BLOCK]
        gcv = r[:, 3 * COL_BLOCK:4 * COL_BLOCK]
        u = gc * h
        for b in range(seqs_per_tile):
            rs = slice(b * rows_per_seq, (b + 1) * rows_per_seq)
            base = b * slot + SUBLANES
            ub = u[rs]
            s_ref[base:base + rows_per_seq, cs] = ub
            um1 = s_ref[base - 1:base - 1 + rows_per_seq, cs]
            um2 = s_ref[base - 2:base - 2 + rows_per_seq, cs]
            y = cw_ref[0:1, cs] * um2 + cw_ref[1:2, cs] * um1 + cw_ref[2:3, cs] * ub
            ycg_ref[rs, cs] = (jax.nn.sigmoid(gcv[rs]) * (gb[rs] * y)).astype(BF16)

    for b in range(seqs_per_tile):
        end = b * slot + SUBLANES + rows_per_seq
        cnew_ref[b] = s_ref[end - 2:end, :]
    if tiles_per_seq > 1:
        end = SUBLANES + rows_per_seq
        s_ref[6:8, :] = s_ref[end - 2:end, :]

    split = W2_SLABS[1][1] - W2_SLABS[1][0]
    sga_ref[:, :split] = jax.nn.sigmoid(take()).astype(BF16)
    sga_ref[:, split:] = jax.nn.sigmoid(take()).astype(BF16)


def _mixer_in(x2d, prefix, nw, w1, w2, cw, bd, qw, kw, *, tile, seq_len):
    n = x2d.shape[0]
    n_seq = n // seq_len
    if tile >= seq_len:
        seqs_per_tile, rows_per_seq, tiles_per_seq = tile // seq_len, seq_len, 1
    else:
        seqs_per_tile, rows_per_seq, tiles_per_seq = 1, tile, seq_len // tile
    assert seqs_per_tile * rows_per_seq == tile and n % tile == 0
    grid = (n_seq // seqs_per_tile, tiles_per_seq)
    row = lambda i, j: (i * tiles_per_seq + j, 0)
    const2 = lambda i, j: (0, 0)
    const3 = lambda i, j: (0, 0, 0)
    seq3 = lambda i, j: (i, 0, 0)
    resident = dict(pipeline_mode=pl.Buffered(1))
    kern = functools.partial(_mixer_in_kernel, seqs_per_tile=seqs_per_tile,
                             rows_per_seq=rows_per_seq, tiles_per_seq=tiles_per_seq)
    return pl.pallas_call(
        kern,
        grid=grid,
        in_specs=[
            pl.BlockSpec((tile, D_MODEL), row),
            pl.BlockSpec((seqs_per_tile, CONV_W - 1, D_CONV), seq3),
            pl.BlockSpec((1, D_MODEL), const2),
            pl.BlockSpec(w1.shape, const3, **resident),
            pl.BlockSpec(w2.shape, const2, **resident),
            pl.BlockSpec(cw.shape, const2),
            pl.BlockSpec(bd.shape, const2),
            pl.BlockSpec(qw.shape, const2),
            pl.BlockSpec(kw.shape, const2),
        ],
        out_specs=[
            pl.BlockSpec((tile, D_CONV), row),
            pl.BlockSpec((tile, D_MODEL), row),
            pl.BlockSpec((tile, D_MODEL), row),
            pl.BlockSpec((tile, KV_DIM), row),
            pl.BlockSpec((tile, KV_DIM), row),
            pl.BlockSpec((seqs_per_tile, CONV_W - 1, D_CONV), seq3),
        ],
        out_shape=[
            jax.ShapeDtypeStruct((n, D_CONV), BF16),
            jax.ShapeDtypeStruct((n, D_MODEL), BF16),
            jax.ShapeDtypeStruct((n, D_MODEL), BF16),
            jax.ShapeDtypeStruct((n, KV_DIM), F32),
            jax.ShapeDtypeStruct((n, KV_DIM), F32),
            jax.ShapeDtypeStruct((n_seq, CONV_W - 1, D_CONV), F32),
        ],
        scratch_shapes=[pltpu.VMEM((seqs_per_tile * (rows_per_seq + SUBLANES), D_CONV), F32)],
        compiler_params=pltpu.CompilerParams(
            dimension_semantics=("arbitrary", "arbitrary"),
            vmem_limit_bytes=VMEM_LIMIT_BYTES),
        name="mixer_in",
    )(x2d, prefix, nw, w1, w2, cw, bd, qw, kw)


def _attention_kernel(q_ref, kp_ref, kc_ref, vp_ref, vc_ref, sink_ref, ycg_ref, sga_ref, o_ref,
                      *, groups, n_new):
    j = pl.program_id(1)
    wide = GROUP * ATT_TOK

    own = ATT_TOK if n_new is None else n_new

    def own_rows(ref, g, cols=slice(None)):
        blk = ref[g * own:(g + 1) * own, cols]
        if own == ATT_TOK:
            return blk
        return jnp.concatenate([blk, jnp.zeros((ATT_TOK - own, blk.shape[1]), blk.dtype)], axis=0)

    def key_blocks(ref_prev, ref_cur, prep):
        own_blk = [prep(own_rows(ref_cur, g)) for g in range(groups)]
        if n_new is None:
            prev_blk = [prep(ref_prev[...])] + own_blk[:-1]
        else:
            prev_blk = [prep(ref_prev[g * WINDOW:(g + 1) * WINDOW, :]) for g in range(groups)]
        return prev_blk, own_blk

    kb_prev, kb_own = key_blocks(kp_ref, kc_ref, lambda t: t.astype(BF16))
    vt_prev, vt_own = key_blocks(vp_ref, vc_ref, lambda t: t.T.astype(BF16))
    ones = jnp.ones((2 * SUBLANES, ATT_KEYS), BF16)
    zeros = jnp.zeros((HEAD_DIM, wide), BF16)
    first_half = lax.broadcasted_iota(jnp.int32, (CHUNK, ATT_TOK), 1) < CHUNK
    has_prev = j > 0

    def scores(g, kv, qt):
        qcat = jnp.concatenate(
            [qt[2 * kv + i // 2][(i % 2) * HEAD_DIM:(i % 2 + 1) * HEAD_DIM] for i in range(GROUP)],
            axis=1)
        rhs = jnp.concatenate([qcat, zeros] if kv % 2 == 0 else [zeros, qcat], axis=0)
        slab = slice((kv // 2) * LANES, (kv // 2 + 1) * LANES)
        kcat = jnp.concatenate([kb_prev[g][:, slab], kb_own[g][:, slab]], axis=0)
        return jnp.dot(kcat, rhs, preferred_element_type=F32)

    def masked(g, s):
        blk = [s[i * CHUNK:(i + 1) * CHUNK] for i in range(ATT_KEYS // CHUNK)]
        if n_new is None:
            if g == 0:
                blk[0] = jnp.where(first_half & has_prev, blk[0], NEG_INF)
                blk[1] = jnp.where(has_prev, blk[1], NEG_INF)
            else:
                blk[0] = jnp.where(first_half, blk[0], NEG_INF)
            blk[3] = jnp.where(first_half, NEG_INF, blk[3])
        else:
            row = lax.broadcasted_iota(jnp.int32, (CHUNK, ATT_TOK), 0)
            blk[2] = jnp.where(row < n_new, blk[2], NEG_INF)
            blk[3] = jnp.full_like(blk[3], NEG_INF)
        return jnp.concatenate(blk, axis=0)

    def softmax_pv(g, kv, s):
        es, sink_terms = [], []
        for i in range(GROUP):
            lanes = slice(i * ATT_TOK, (i + 1) * ATT_TOK)
            si = masked(g, s[:, lanes])
            sink = sink_ref[kv:kv + 1, lanes]
            m = jnp.maximum(jnp.max(si, axis=0, keepdims=True), sink)
            es.append(jnp.exp2(si - m).astype(BF16))
            sink_terms.append(jnp.exp2(sink - m))
        e = jnp.concatenate(es, axis=1)
        vrows = slice(kv * HEAD_DIM, (kv + 1) * HEAD_DIM)
        lhs = jnp.concatenate([vt_prev[g][vrows], vt_own[g][vrows]], axis=1)
        oe = jnp.dot(jnp.concatenate([lhs, ones], axis=0), e, preferred_element_type=F32)
        den = oe[HEAD_DIM:HEAD_DIM + 1] + jnp.concatenate(sink_terms, axis=1)
        return oe[:HEAD_DIM] * (1.0 / den)

    def q_transposed(g):
        return [own_rows(q_ref, g, slice(p * LANES, (p + 1) * LANES)).astype(F32).T.astype(BF16)
                for p in range(D_MODEL // LANES)]

    def merge(g, outs):
        rows = slice(g * own, (g + 1) * own)
        for p in range(D_MODEL // LANES):
            cs = slice(p * LANES, (p + 1) * LANES)
            pair = []
            for head in (2 * p, 2 * p + 1):
                i = head % GROUP
                pair.append(outs[head // GROUP][:, i * ATT_TOK:(i + 1) * ATT_TOK])
            a = jnp.concatenate(pair, axis=0).T[:own].astype(BF16)
            o_ref[rows, cs] = ycg_ref[rows, cs] + sga_ref[rows, cs] * a

    units = [(g, kv) for g in range(groups) for kv in range(N_KV_HEADS)]
    qts = {}
    pending = []

    def issue(n):
        g2, kv2 = units[n]
        if g2 not in qts:
            qts[g2] = q_transposed(g2)
        pending.append(scores(g2, kv2, qts[g2]))

    for n in range(min(SCORE_LOOKAHEAD, len(units))):
        issue(n)
    outs = []
    for n, (g, kv) in enumerate(units):
        if n + SCORE_LOOKAHEAD < len(units):
            issue(n + SCORE_LOOKAHEAD)
        outs.append(softmax_pv(g, kv, pending.pop(0)))
        if kv == N_KV_HEADS - 1:
            merge(g, outs)
            outs = []


def _attention(qs, k_prev_arr, k_cur_arr, v_prev_arr, v_cur_arr, sink_w, ycg, sga,
               *, n_seq, tile, tiles_per_seq, prev_map, n_new):
    n = qs.shape[0]
    groups = tile // (ATT_TOK if n_new is None else n_new)
    prev_rows = WINDOW if n_new is None else groups * WINDOW
    cur = lambda i, j: (i * tiles_per_seq + j, 0)
    const2 = lambda i, j: (0, 0)
    kern = functools.partial(_attention_kernel, groups=groups, n_new=n_new)
    return pl.pallas_call(
        kern,
        grid=(n_seq, tiles_per_seq),
        in_specs=[
            pl.BlockSpec((tile, D_MODEL), cur),
            pl.BlockSpec((prev_rows, KV_DIM), prev_map),
            pl.BlockSpec((tile, KV_DIM), cur),
            pl.BlockSpec((prev_rows, KV_DIM), prev_map),
            pl.BlockSpec((tile, KV_DIM), cur),
            pl.BlockSpec(sink_w.shape, const2),
            pl.BlockSpec((tile, D_MODEL), cur),
            pl.BlockSpec((tile, D_MODEL), cur),
        ],
        out_specs=pl.BlockSpec((tile, D_MODEL), cur),
        out_shape=jax.ShapeDtypeStruct((n, D_MODEL), BF16),
        compiler_params=pltpu.CompilerParams(
            dimension_semantics=("arbitrary", "arbitrary"),
            vmem_limit_bytes=VMEM_LIMIT_BYTES),
        name="attention",
    )(qs, k_prev_arr, k_cur_arr, v_prev_arr, v_cur_arr, sink_w, ycg, sga)


def _out_ffn_kernel(m_ref, x_ref, wo_ref, nw_ref, wg_ref, wu_ref, wd_ref, y_ref):
    x1 = x_ref[...] + jnp.dot(m_ref[...], wo_ref[...], preferred_element_type=F32)
    xn = (x1 * _rms_scale(x1) * nw_ref[...]).astype(BF16)
    acc = x1
    for lo, hi in FF_BLOCKS:
        g = jnp.dot(xn, wg_ref[:, lo:hi], preferred_element_type=F32)
        u = jnp.dot(xn, wu_ref[:, lo:hi], preferred_element_type=F32)
        a = (jax.nn.silu(g) * u).astype(BF16)
        acc = acc + jnp.dot(a, wd_ref[lo:hi, :], preferred_element_type=F32)
    y_ref[...] = acc


def _out_ffn(merged, x2d, wo, nw, wg, wu, wd, *, tile):
    n = x2d.shape[0]
    row = lambda i: (i, 0)
    const2 = lambda i: (0, 0)
    resident = dict(pipeline_mode=pl.Buffered(1))
    return pl.pallas_call(
        _out_ffn_kernel,
        grid=(n // tile,),
        in_specs=[
            pl.BlockSpec((tile, D_MODEL), row),
            pl.BlockSpec((tile, D_MODEL), row),
            pl.BlockSpec(wo.shape, const2, **resident),
            pl.BlockSpec((1, D_MODEL), const2),
            pl.BlockSpec(wg.shape, const2, **resident),
            pl.BlockSpec(wu.shape, const2, **resident),
            pl.BlockSpec(wd.shape, const2, **resident),
        ],
        out_specs=pl.BlockSpec((tile, D_MODEL), row),
        out_shape=jax.ShapeDtypeStruct((n, D_MODEL), F32),
        compiler_params=pltpu.CompilerParams(
            dimension_semantics=("arbitrary",),
            vmem_limit_bytes=VMEM_LIMIT_BYTES),
        name="out_ffn",
    )(merged, x2d, wo, nw, wg, wu, wd)


def _layer(xp, xs, state_conv, cache_k, cache_v, norm_mix_w, w_in, conv_w, q_norm_w, k_norm_w,
           sinks, w_out, norm_ffn_w, w_gate, w_up, w_down):
    b_p, t_p, _ = xp.shape
    b_s, t_s, _ = xs.shape

    hs, bs, cs_, qs_, ks_, vs_, gcs, gas = (0, D_CONV, 2 * D_CONV, 3 * D_CONV, 3 * D_CONV + D_MODEL,
                                            3 * D_CONV + D_MODEL + KV_DIM,
                                            3 * D_CONV + D_MODEL + 2 * KV_DIM,
                                            4 * D_CONV + D_MODEL + 2 * KV_DIM)
    w_in_b = w_in.astype(BF16)
    w1 = jnp.stack([
        jnp.concatenate([w_in_b[:, s + cb * COL_BLOCK:s + (cb + 1) * COL_BLOCK] for s in (hs, bs, cs_, gcs)],
                        axis=1)
        for cb in range(N_COL_BLOCKS)])
    w2 = jnp.concatenate([w_in_b[:, qs_:qs_ + D_MODEL], w_in_b[:, ks_:ks_ + KV_DIM],
                          w_in_b[:, vs_:vs_ + KV_DIM], w_in_b[:, gas:gas + D_MODEL]], axis=1)
    head_id = jnp.arange(MXU_DIM) // HEAD_DIM
    bd = ((head_id[:, None] == head_id[None, :]) * (1.0 / HEAD_DIM)).astype(BF16)
    nw = norm_mix_w.reshape(1, D_MODEL)
    qw = (jnp.tile(q_norm_w, N_HEADS) * (SCALE * LOG2E)).reshape(1, D_MODEL)
    kw = jnp.tile(k_norm_w, N_KV_HEADS).reshape(1, KV_DIM)
    sink_w = jnp.broadcast_to((sinks.astype(F32) * LOG2E).reshape(N_KV_HEADS, GROUP, 1),
                              (N_KV_HEADS, GROUP, ATT_TOK)).reshape(N_KV_HEADS, GROUP * ATT_TOK)
    wo = w_out.astype(BF16)
    nfw = norm_ffn_w.reshape(1, D_MODEL)
    wg, wu, wd = w_gate.astype(BF16), w_up.astype(BF16), w_down.astype(BF16)

    xp2 = xp.reshape(b_p * t_p, D_MODEL)
    zero_prefix = jnp.zeros((b_p, CONV_W - 1, D_CONV), F32)
    ycg, q, sga, kh, v, cnew_p = _mixer_in(xp2, zero_prefix, nw, w1, w2, conv_w, bd, qw, kw,
                                           tile=512, seq_len=t_p)
    att_tile = 2048
    blocks_per_tile = att_tile // WINDOW
    blocks_per_seq = t_p // WINDOW
    merged = _attention(
        q, kh, kh, v, v, sink_w, ycg, sga, n_seq=b_p, tile=att_tile, tiles_per_seq=t_p // att_tile,
        prev_map=lambda i, j: (i * blocks_per_seq + jnp.maximum(j * blocks_per_tile - 1, 0), 0),
        n_new=None)
    yp = _out_ffn(merged, xp2, wo, nfw, wg, wu, wd, tile=512).reshape(b_p, t_p, D_MODEL)
    tail = lambda a: a.reshape(b_p, t_p, KV_DIM)[:, -WINDOW:].reshape(b_p, WINDOW, N_KV_HEADS, HEAD_DIM)
    k_p, v_p = tail(kh), tail(v)

    xs2 = xs.reshape(b_s * t_s, D_MODEL)
    ycg_s, q_s, sga_s, kh_s, v_s, cnew_s = _mixer_in(xs2, state_conv, nw, w1, w2, conv_w, bd, qw, kw,
                                                     tile=b_s * t_s, seq_len=t_s)
    cache_len = cache_k.shape[1]
    merged_s = _attention(
        q_s, cache_k.reshape(b_s * cache_len, KV_DIM), kh_s,
        cache_v.reshape(b_s * cache_len, KV_DIM), v_s, sink_w, ycg_s, sga_s,
        n_seq=b_s // SAMPLE_SEQS_PER_STEP, tile=SAMPLE_SEQS_PER_STEP * t_s, tiles_per_seq=1,
        prev_map=lambda i, j: (i, 0), n_new=t_s)
    ys = _out_ffn(merged_s, xs2, wo, nfw, wg, wu, wd, tile=b_s * t_s).reshape(b_s, t_s, D_MODEL)
    k_s = kh_s.reshape(b_s, t_s, N_KV_HEADS, HEAD_DIM)
    v_s = v_s.reshape(b_s, t_s, N_KV_HEADS, HEAD_DIM)
    return yp, ys, cnew_p, k_p, v_p, cnew_s, k_s, v_s


def kernel(x_prompt, x_sample, state_conv, cache_k, cache_v, norm_mix_w, w_in, conv_w, q_norm_w, k_norm_w, sinks, w_out, norm_ffn_w, w_gate, w_up, w_down):
    depth = w_in.shape[0]
    assert cache_k.shape[2] == WINDOW, "sample attention assumes a full window of cached rows"
    yp, ys = x_prompt, x_sample
    outs = [[] for _ in range(6)]
    for l in range(depth):
        yp, ys, *rest = _layer(
            yp, ys, state_conv[l], cache_k[l], cache_v[l], norm_mix_w[l], w_in[l], conv_w[l],
            q_norm_w[l], k_norm_w[l], sinks[l], w_out[l], norm_ffn_w[l], w_gate[l], w_up[l], w_down[l])
        for acc, val in zip(outs, rest):
            acc.append(val)
    return (yp, ys) + tuple(jnp.stack(o) for o in outs)
```

```python
import functools

import jax
import jax.numpy as jnp
from jax import lax
from jax.experimental import pallas as pl
from jax.experimental.pallas import tpu as pltpu

D_MODEL = 1024
HEAD_DIM = 64
N_HEADS = 16
N_KV_HEADS = 4
GROUP = N_HEADS // N_KV_HEADS
KV_DIM = N_KV_HEADS * HEAD_DIM
D_CONV = D_MODEL
CONV_W = 3
D_FF = 2816
CHUNK = 64
WINDOW = 128
EPS = 1e-6
NEG_INF = -1e30
SCALE = HEAD_DIM ** -0.5

LANES = 128
SUBLANES = 8
MXU_DIM = 256
VMEM_LIMIT_BYTES = 56 * 1024 * 1024

COL_BLOCK = MXU_DIM
N_COL_BLOCKS = D_CONV // COL_BLOCK
ATT_TOK = 2 * CHUNK
ATT_KEYS = ATT_TOK + WINDOW
FF_BLOCKS = ((0, D_FF),)
_QKV = D_MODEL + 2 * KV_DIM
W2_SLABS = ((0, _QKV), (_QKV, _QKV + 3 * MXU_DIM), (_QKV + 3 * MXU_DIM, _QKV + D_MODEL))
LOG2E = 1.4426950408889634
SAMPLE_SEQS_PER_STEP = 4
SLAB_LOOKAHEAD = 6
SCORE_LOOKAHEAD = 2

BF16 = jnp.bfloat16
F32 = jnp.float32


def _rms_scale(x):
    return lax.rsqrt(jnp.mean(x * x, axis=-1, keepdims=True) + EPS)


def _mixer_in_kernel(x_ref, pre_ref, nw_ref, w1_ref, w2_ref, cw_ref, bd_ref, qw_ref, kw_ref,
                     ycg_ref, qs_ref, sga_ref, kh_ref, v_ref, cnew_ref, s_ref,
                     *, seqs_per_tile, rows_per_seq, tiles_per_seq):
    j = pl.program_id(1)
    slot = rows_per_seq + SUBLANES

    def load_prefix():
        for b in range(seqs_per_tile):
            s_ref[b * slot + 6:b * slot + 8, :] = pre_ref[b]

    if tiles_per_seq == 1:
        load_prefix()
    else:
        pl.when(j == 0)(load_prefix)

    x = x_ref[...]
    xn = (x * _rms_scale(x) * nw_ref[...]).astype(BF16)

    def project(slab):
        if slab < N_COL_BLOCKS:
            return jnp.dot(xn, w1_ref[slab], preferred_element_type=F32)
        lo, hi = W2_SLABS[slab - N_COL_BLOCKS]
        return jnp.dot(xn, w2_ref[:, lo:hi], preferred_element_type=F32)

    order = [N_COL_BLOCKS] + list(range(N_COL_BLOCKS)) + [N_COL_BLOCKS + 1, N_COL_BLOCKS + 2]
    queue = []

    def take():
        while order and len(queue) <= SLAB_LOOKAHEAD:
            queue.append(project(order.pop(0)))
        return queue.pop(0)

    qkv = take()
    bd = bd_ref[...]

    def head_scale(t):
        ms = jnp.dot((t * t).astype(BF16), bd, preferred_element_type=F32)
        return lax.rsqrt(ms + EPS)

    for c in range(D_MODEL // MXU_DIM):
        cs = slice(c * MXU_DIM, (c + 1) * MXU_DIM)
        qb = qkv[:, cs]
        qs_ref[:, cs] = (qb * head_scale(qb) * qw_ref[:, cs]).astype(BF16)
    k = qkv[:, D_MODEL:D_MODEL + KV_DIM]
    kh_ref[...] = k * head_scale(k) * kw_ref[...]
    v_ref[...] = qkv[:, D_MODEL + KV_DIM:D_MODEL + 2 * KV_DIM]

    for cb in range(N_COL_BLOCKS):
        cs = slice(cb * COL_BLOCK, (cb + 1) * COL_BLOCK)
        r = take()
        h = r[:, 0 * COL_BLOCK:1 * COL_BLOCK]
        gb = r[:, 1 * COL_BLOCK:2 * COL_BLOCK]
        gc = r[:, 2 * COL_BLOCK:3 * COL_BLOCK]
        gcv = r[:, 3 * COL_BLOCK:4 * COL_BLOCK]
        u = gc * h
        for b in range(seqs_per_tile):
            rs = slice(b * rows_per_seq, (b + 1) * rows_per_seq)
            base = b * slot + SUBLANES
            ub = u[rs]
            s_ref[base:base + rows_per_seq, cs] = ub
            um1 = s_ref[base - 1:base - 1 + rows_per_seq, cs]
            um2 = s_ref[base - 2:base - 2 + rows_per_seq, cs]
            y = cw_ref[0:1, cs] * um2 + cw_ref[1:2, cs] * um1 + cw_ref[2:3, cs] * ub
            ycg_ref[rs, cs] = (jax.nn.sigmoid(gcv[rs]) * (gb[rs] * y)).astype(BF16)

    for b in range(seqs_per_tile):
        end = b * slot + SUBLANES + rows_per_seq
        cnew_ref[b] = s_ref[end - 2:end, :]
    if tiles_per_seq > 1:
        end = SUBLANES + rows_per_seq
        s_ref[6:8, :] = s_ref[end - 2:end, :]

    split = W2_SLABS[1][1] - W2_SLABS[1][0]
    sga_ref[:, :split] = jax.nn.sigmoid(take()).astype(BF16)
    sga_ref[:, split:] = jax.nn.sigmoid(take()).astype(BF16)


def _mixer_in(x2d, prefix, nw, w1, w2, cw, bd, qw, kw, *, tile, seq_len):
    n = x2d.shape[0]
    n_seq = n // seq_len
    if tile >= seq_len:
        seqs_per_tile, rows_per_seq, tiles_per_seq = tile // seq_len, seq_len, 1
    else:
        seqs_per_tile, rows_per_seq, tiles_per_seq = 1, tile, seq_len // tile
    assert seqs_per_tile * rows_per_seq == tile and n % tile == 0
    grid = (n_seq // seqs_per_tile, tiles_per_seq)
    row = lambda i, j: (i * tiles_per_seq + j, 0)
    const2 = lambda i, j: (0, 0)
    const3 = lambda i, j: (0, 0, 0)
    seq3 = lambda i, j: (i, 0, 0)
    resident = dict(pipeline_mode=pl.Buffered(1))
    kern = functools.partial(_mixer_in_kernel, seqs_per_tile=seqs_per_tile,
                             rows_per_seq=rows_per_seq, tiles_per_seq=tiles_per_seq)
    return pl.pallas_call(
        kern,
        grid=grid,
        in_specs=[
            pl.BlockSpec((tile, D_MODEL), row),
            pl.BlockSpec((seqs_per_tile, CONV_W - 1, D_CONV), seq3),
            pl.BlockSpec((1, D_MODEL), const2),
            pl.BlockSpec(w1.shape, const3, **resident),
            pl.BlockSpec(w2.shape, const2, **resident),
            pl.BlockSpec(cw.shape, const2),
            pl.BlockSpec(bd.shape, const2),
            pl.BlockSpec(qw.shape, const2),
            pl.BlockSpec(kw.shape, const2),
        ],
        out_specs=[
            pl.BlockSpec((tile, D_CONV), row),
            pl.BlockSpec((tile, D_MODEL), row),
            pl.BlockSpec((tile, D_MODEL), row),
            pl.BlockSpec((tile, KV_DIM), row),
            pl.BlockSpec((tile, KV_DIM), row),
            pl.BlockSpec((seqs_per_tile, CONV_W - 1, D_CONV), seq3),
        ],
        out_shape=[
            jax.ShapeDtypeStruct((n, D_CONV), BF16),
            jax.ShapeDtypeStruct((n, D_MODEL), BF16),
            jax.ShapeDtypeStruct((n, D_MODEL), BF16),
            jax.ShapeDtypeStruct((n, KV_DIM), F32),
            jax.ShapeDtypeStruct((n, KV_DIM), F32),
            jax.ShapeDtypeStruct((n_seq, CONV_W - 1, D_CONV), F32),
        ],
        scratch_shapes=[pltpu.VMEM((seqs_per_tile * (rows_per_seq + SUBLANES), D_CONV), F32)],
        compiler_params=pltpu.CompilerParams(
            dimension_semantics=("arbitrary", "arbitrary"),
            vmem_limit_bytes=VMEM_LIMIT_BYTES),
        name="mixer_in",
    )(x2d, prefix, nw, w1, w2, cw, bd, qw, kw)


def _attention_kernel(q_ref, kp_ref, kc_ref, vp_ref, vc_ref, sink_ref, ycg_ref, sga_ref, o_ref,
                      *, groups, n_new):
    j = pl.program_id(1)
    wide = GROUP * ATT_TOK

    own = ATT_TOK if n_new is None else n_new

    def own_rows(ref, g, cols=slice(None)):
        blk = ref[g * own:(g + 1) * own, cols]
        if own == ATT_TOK:
            return blk
        return jnp.concatenate([blk, jnp.zeros((ATT_TOK - own, blk.shape[1]), blk.dtype)], axis=0)

    def key_blocks(ref_prev, ref_cur, prep):
        own_blk = [prep(own_rows(ref_cur, g)) for g in range(groups)]
        if n_new is None:
            prev_blk = [prep(ref_prev[...])] + own_blk[:-1]
        else:
            prev_blk = [prep(ref_prev[g * WINDOW:(g + 1) * WINDOW, :]) for g in range(groups)]
        return prev_blk, own_blk

    kb_prev, kb_own = key_blocks(kp_ref, kc_ref, lambda t: t.astype(BF16))
    vt_prev, vt_own = key_blocks(vp_ref, vc_ref, lambda t: t.T.astype(BF16))
    ones = jnp.ones((2 * SUBLANES, ATT_KEYS), BF16)
    zeros = jnp.zeros((HEAD_DIM, wide), BF16)
    first_half = lax.broadcasted_iota(jnp.int32, (CHUNK, ATT_TOK), 1) < CHUNK
    has_prev = j > 0

    def scores(g, kv, qt):
        qcat = jnp.concatenate(
            [qt[2 * kv + i // 2][(i % 2) * HEAD_DIM:(i % 2 + 1) * HEAD_DIM] for i in range(GROUP)],
            axis=1)
        rhs = jnp.concatenate([qcat, zeros] if kv % 2 == 0 else [zeros, qcat], axis=0)
        slab = slice((kv // 2) * LANES, (kv // 2 + 1) * LANES)
        kcat = jnp.concatenate([kb_prev[g][:, slab], kb_own[g][:, slab]], axis=0)
        return jnp.dot(kcat, rhs, preferred_element_type=F32)

    def masked(g, s):
        blk = [s[i * CHUNK:(i + 1) * CHUNK] for i in range(ATT_KEYS // CHUNK)]
        if n_new is None:
            if g == 0:
                blk[0] = jnp.where(first_half & has_prev, blk[0], NEG_INF)
                blk[1] = jnp.where(has_prev, blk[1], NEG_INF)
            else:
                blk[0] = jnp.where(first_half, blk[0], NEG_INF)
            blk[3] = jnp.where(first_half, NEG_INF, blk[3])
        else:
            row = lax.broadcasted_iota(jnp.int32, (CHUNK, ATT_TOK), 0)
            blk[2] = jnp.where(row < n_new, blk[2], NEG_INF)
            blk[3] = jnp.full_like(blk[3], NEG_INF)
        return jnp.concatenate(blk, axis=0)

    def softmax_pv(g, kv, s):
        es, sink_terms = [], []
        for i in range(GROUP):
            lanes = slice(i * ATT_TOK, (i + 1) * ATT_TOK)
            si = masked(g, s[:, lanes])
            sink = sink_ref[kv:kv + 1, lanes]
            m = jnp.maximum(jnp.max(si, axis=0, keepdims=True), sink)
            es.append(jnp.exp2(si - m).astype(BF16))
            sink_terms.append(jnp.exp2(sink - m))
        e = jnp.concatenate(es, axis=1)
        vrows = slice(kv * HEAD_DIM, (kv + 1) * HEAD_DIM)
        lhs = jnp.concatenate([vt_prev[g][vrows], vt_own[g][vrows]], axis=1)
        oe = jnp.dot(jnp.concatenate([lhs, ones], axis=0), e, preferred_element_type=F32)
        den = oe[HEAD_DIM:HEAD_DIM + 1] + jnp.concatenate(sink_terms, axis=1)
        return oe[:HEAD_DIM] * (1.0 / den)

    def q_transposed(g):
        return [own_rows(q_ref, g, slice(p * LANES, (p + 1) * LANES)).astype(F32).T.astype(BF16)
                for p in range(D_MODEL // LANES)]

    def merge(g, outs):
        rows = slice(g * own, (g + 1) * own)
        for p in range(D_MODEL // LANES):
            cs = slice(p * LANES, (p + 1) * LANES)
            pair = []
            for head in (2 * p, 2 * p + 1):
                i = head % GROUP
                pair.append(outs[head // GROUP][:, i * ATT_TOK:(i + 1) * ATT_TOK])
            a = jnp.concatenate(pair, axis=0).T[:own].astype(BF16)
            o_ref[rows, cs] = ycg_ref[rows, cs] + sga_ref[rows, cs] * a

    units = [(g, kv) for g in range(groups) for kv in range(N_KV_HEADS)]
    qts = {}
    pending = []

    def issue(n):
        g2, kv2 = units[n]
        if g2 not in qts:
            qts[g2] = q_transposed(g2)
        pending.append(scores(g2, kv2, qts[g2]))

    for n in range(min(SCORE_LOOKAHEAD, len(units))):
        issue(n)
    outs = []
    for n, (g, kv) in enumerate(units):
        if n + SCORE_LOOKAHEAD < len(units):
            issue(n + SCORE_LOOKAHEAD)
        outs.append(softmax_pv(g, kv, pending.pop(0)))
        if kv == N_KV_HEADS - 1:
            merge(g, outs)
            outs = []


def _attention(qs, k_prev_arr, k_cur_arr, v_prev_arr, v_cur_arr, sink_w, ycg, sga,
               *, n_seq, tile, tiles_per_seq, prev_map, n_new):
    n = qs.shape[0]
    groups = tile // (ATT_TOK if n_new is None else n_new)
    prev_rows = WINDOW if n_new is None else groups * WINDOW
    cur = lambda i, j: (i * tiles_per_seq + j, 0)
    const2 = lambda i, j: (0, 0)
    kern = functools.partial(_attention_kernel, groups=groups, n_new=n_new)
    return pl.pallas_call(
        kern,
        grid=(n_seq, tiles_per_seq),
        in_specs=[
            pl.BlockSpec((tile, D_MODEL), cur),
            pl.BlockSpec((prev_rows, KV_DIM), prev_map),
            pl.BlockSpec((tile, KV_DIM), cur),
            pl.BlockSpec((prev_rows, KV_DIM), prev_map),
            pl.BlockSpec((tile, KV_DIM), cur),
            pl.BlockSpec(sink_w.shape, const2),
            pl.BlockSpec((tile, D_MODEL), cur),
            pl.BlockSpec((tile, D_MODEL), cur),
        ],
        out_specs=pl.BlockSpec((tile, D_MODEL), cur),
        out_shape=jax.ShapeDtypeStruct((n, D_MODEL), BF16),
        compiler_params=pltpu.CompilerParams(
            dimension_semantics=("arbitrary", "arbitrary"),
            vmem_limit_bytes=VMEM_LIMIT_BYTES),
        name="attention",
    )(qs, k_prev_arr, k_cur_arr, v_prev_arr, v_cur_arr, sink_w, ycg, sga)


def _out_ffn_kernel(m_ref, x_ref, wo_ref, nw_ref, wg_ref, wu_ref, wd_ref, y_ref):
    x1 = x_ref[...] + jnp.dot(m_ref[...], wo_ref[...], preferred_element_type=F32)
    xn = (x1 * _rms_scale(x1) * nw_ref[...]).astype(BF16)
    acc = x1
    for lo, hi in FF_BLOCKS:
        g = jnp.dot(xn, wg_ref[:, lo:hi], preferred_element_type=F32)
        u = jnp.dot(xn, wu_ref[:, lo:hi], preferred_element_type=F32)
        a = (jax.nn.silu(g) * u).astype(BF16)
        acc = acc + jnp.dot(a, wd_ref[lo:hi, :], preferred_element_type=F32)
    y_ref[...] = acc


def _out_ffn(merged, x2d, wo, nw, wg, wu, wd, *, tile):
    n = x2d.shape[0]
    row = lambda i: (i, 0)
    const2 = lambda i: (0, 0)
    resident = dict(pipeline_mode=pl.Buffered(1))
    return pl.pallas_call(
        _out_ffn_kernel,
        grid=(n // tile,),
        in_specs=[
            pl.BlockSpec((tile, D_MODEL), row),
            pl.BlockSpec((tile, D_MODEL), row),
            pl.BlockSpec(wo.shape, const2, **resident),
            pl.BlockSpec((1, D_MODEL), const2),
            pl.BlockSpec(wg.shape, const2, **resident),
            pl.BlockSpec(wu.shape, const2, **resident),
            pl.BlockSpec(wd.shape, const2, **resident),
        ],
        out_specs=pl.BlockSpec((tile, D_MODEL), row),
        out_shape=jax.ShapeDtypeStruct((n, D_MODEL), F32),
        compiler_params=pltpu.CompilerParams(
            dimension_semantics=("arbitrary",),
            vmem_limit_bytes=VMEM_LIMIT_BYTES),
        name="out_ffn",
    )(merged, x2d, wo, nw, wg, wu, wd)


def _layer(xp, xs, state_conv, cache_k, cache_v, norm_mix_w, w_in, conv_w, q_norm_w, k_norm_w,
           sinks, w_out, norm_ffn_w, w_gate, w_up, w_down):
    b_p, t_p, _ = xp.shape
    b_s, t_s, _ = xs.shape

    hs, bs, cs_, qs_, ks_, vs_, gcs, gas = (0, D_CONV, 2 * D_CONV, 3 * D_CONV, 3 * D_CONV + D_MODEL,
                                            3 * D_CONV + D_MODEL + KV_DIM,
                                            3 * D_CONV + D_MODEL + 2 * KV_DIM,
                                            4 * D_CONV + D_MODEL + 2 * KV_DIM)
    w_in_b = w_in.astype(BF16)
    w1 = jnp.stack([
        jnp.concatenate([w_in_b[:, s + cb * COL_BLOCK:s + (cb + 1) * COL_BLOCK] for s in (hs, bs, cs_, gcs)],
                        axis=1)
        for cb in range(N_COL_BLOCKS)])
    w2 = jnp.concatenate([w_in_b[:, qs_:qs_ + D_MODEL], w_in_b[:, ks_:ks_ + KV_DIM],
                          w_in_b[:, vs_:vs_ + KV_DIM], w_in_b[:, gas:gas + D_MODEL]], axis=1)
    head_id = jnp.arange(MXU_DIM) // HEAD_DIM
    bd = ((head_id[:, None] == head_id[None, :]) * (1.0 / HEAD_DIM)).astype(BF16)
    nw = norm_mix_w.reshape(1, D_MODEL)
    qw = (jnp.tile(q_norm_w, N_HEADS) * (SCALE * LOG2E)).reshape(1, D_MODEL)
    kw = jnp.tile(k_norm_w, N_KV_HEADS).reshape(1, KV_DIM)
    sink_w = jnp.broadcast_to((sinks.astype(F32) * LOG2E).reshape(N_KV_HEADS, GROUP, 1),
                              (N_KV_HEADS, GROUP, ATT_TOK)).reshape(N_KV_HEADS, GROUP * ATT_TOK)
    wo = w_out.astype(BF16)
    nfw = norm_ffn_w.reshape(1, D_MODEL)
    wg, wu, wd = w_gate.astype(BF16), w_up.astype(BF16), w_down.astype(BF16)

    xp2 = xp.reshape(b_p * t_p, D_MODEL)
    zero_prefix = jnp.zeros((b_p, CONV_W - 1, D_CONV), F32)
    ycg, q, sga, kh, v, cnew_p = _mixer_in(xp2, zero_prefix, nw, w1, w2, conv_w, bd, qw, kw,
                                           tile=512, seq_len=t_p)
    att_tile = 2048
    blocks_per_tile = att_tile // WINDOW
    blocks_per_seq = t_p // WINDOW
    merged = _attention(
        q, kh, kh, v, v, sink_w, ycg, sga, n_seq=b_p, tile=att_tile, tiles_per_seq=t_p // att_tile,
        prev_map=lambda i, j: (i * blocks_per_seq + jnp.maximum(j * blocks_per_tile - 1, 0), 0),
        n_new=None)
    yp = _out_ffn(merged, xp2, wo, nfw, wg, wu, wd, tile=512).reshape(b_p, t_p, D_MODEL)
    tail = lambda a: a.reshape(b_p, t_p, KV_DIM)[:, -WINDOW:].reshape(b_p, WINDOW, N_KV_HEADS, HEAD_DIM)
    k_p, v_p = tail(kh), tail(v)

    xs2 = xs.reshape(b_s * t_s, D_MODEL)
    ycg_s, q_s, sga_s, kh_s, v_s, cnew_s = _mixer_in(xs2, state_conv, nw, w1, w2, conv_w, bd, qw, kw,
                                                     tile=b_s * t_s, seq_len=t_s)
    cache_len = cache_k.shape[1]
    merged_s = _attention(
        q_s, cache_k.reshape(b_s * cache_len, KV_DIM), kh_s,
        cache_v.reshape(b_s * cache_len, KV_DIM), v_s, sink_w, ycg_s, sga_s,
        n_seq=b_s // SAMPLE_SEQS_PER_STEP, tile=SAMPLE_SEQS_PER_STEP * t_s, tiles_per_seq=1,
        prev_map=lambda i, j: (i, 0), n_new=t_s)
    ys = _out_ffn(merged_s, xs2, wo, nfw, wg, wu, wd, tile=b_s * t_s).reshape(b_s, t_s, D_MODEL)
    k_s = kh_s.reshape(b_s, t_s, N_KV_HEADS, HEAD_DIM)
    v_s = v_s.reshape(b_s, t_s, N_KV_HEADS, HEAD_DIM)
    return yp, ys, cnew_p, k_p, v_p, cnew_s, k_s, v_s


def kernel(x_prompt, x_sample, state_conv, cache_k, cache_v, norm_mix_w, w_in, conv_w, q_norm_w, k_norm_w, sinks, w_out, norm_ffn_w, w_gate, w_up, w_down):
    depth = w_in.shape[0]
    assert cache_k.shape[2] == WINDOW, "sample attention assumes a full window of cached rows"
    yp, ys = x_prompt, x_sample
    outs = [[] for _ in range(6)]
    for l in range(depth):
        yp, ys, *rest = _layer(
            yp, ys, state_conv[l], cache_k[l], cache_v[l], norm_mix_w[l], w_in[l], conv_w[l],
            q_norm_w[l], k_norm_w[l], sinks[l], w_out[l], norm_ffn_w[l], w_gate[l], w_up[l], w_down[l])
        for acc, val in zip(outs, rest):
            acc.append(val)
    return (yp, ys) + tuple(jnp.stack(o) for o in outs)
```
